```python
import jax, jax.numpy as jnp
from jax import lax
import numpy as np

D_MODEL = 1024
BATCH = 32
SEQ = 2048
DEPTH = 2
DEC_BATCH = 8
DEC_SEQ = 64
PAST_LEN = 1024

CHUNK = 64
PLE_DIM = 256
D_FF = 2816
EPS = 1e-6
GLA_HEADS = 4
GLA_DK = D_MODEL // 2 // GLA_HEADS
GLA_DV = D_MODEL // GLA_HEADS
GLA_RANK = 16
GLA_TAU = 16.0
SWA_HEADS = 16
SWA_KV_HEADS = 4
SWA_GROUP = SWA_HEADS // SWA_KV_HEADS
SWA_HD = D_MODEL // SWA_HEADS
WINDOW = 128
WIN_CHUNKS = WINDOW // CHUNK
SPLITS = (D_MODEL, D_MODEL,
          GLA_HEADS * GLA_DK, GLA_HEADS * GLA_DK, GLA_HEADS * GLA_DV, GLA_HEADS * GLA_DV, GLA_RANK,
          SWA_HEADS * SWA_HD, SWA_KV_HEADS * SWA_HD, SWA_KV_HEADS * SWA_HD)
D_IN = sum(SPLITS)

kernel_name = 'hybrid_gla_swa_sink_macaron_stream_step'


def rmsnorm(x, g):
    xf = x.astype(jnp.float32)
    y = xf * lax.rsqrt(jnp.mean(xf * xf, axis=-1, keepdims=True) + EPS)
    return (y * g.astype(jnp.float32)).astype(x.dtype)


def swiglu(x, w_up, w_down):
    gate, up = jnp.split(x @ w_up, 2, axis=-1)
    return (jax.nn.silu(gate) * up) @ w_down


def gla_scan(q, k, v, log_a, s0):
    B, L, H, DK = q.shape
    C = min(CHUNK, L)
    n = L // C

    def to_blocks(t):
        return jnp.moveaxis(t.reshape(B, n, C, H, t.shape[-1]), 1, 0)

    mask = jnp.tril(jnp.ones((C, C), dtype=bool))[None, :, :, None, None]

    def step(s, inp):
        qi, ki, vi, ai = inp
        b = jnp.cumsum(ai, axis=1)
        diff = b[:, :, None] - b[:, None, :]
        decay = jnp.where(mask, jnp.exp(jnp.where(mask, diff, 0.0)), 0.0)
        scores = jnp.einsum('bijhd,bjhd->bhij', qi[:, :, None] * decay, ki)
        o_intra = jnp.einsum('bhij,bjhv->bihv', scores, vi)
        o_inter = jnp.einsum('bihd,bhdv->bihv', qi * jnp.exp(b), s)
        b_last = b[:, -1]
        k_dec = ki * jnp.exp(b_last[:, None] - b)
        s_new = jnp.exp(b_last)[..., None] * s + jnp.einsum('bjhd,bjhv->bhdv', k_dec, vi)
        return s_new, o_intra + o_inter

    s_fin, o = lax.scan(step, s0, (to_blocks(q), to_blocks(k), to_blocks(v), to_blocks(log_a)))
    return jnp.moveaxis(o, 0, 1).reshape(B, L, H, v.shape[-1]), s_fin


def sink_attention(q, k, v, sinks, valid):
    sc = jnp.einsum('nihgd,njhd->nhgij', q, k).astype(jnp.float32) * SWA_HD ** -0.5
    if valid is not None:
        sc = jnp.where(valid[:, None, None, None, :], sc, -jnp.inf)
    sink = sinks.astype(jnp.float32).reshape(1, SWA_KV_HEADS, SWA_GROUP, 1, 1)
    m = jnp.maximum(sc.max(axis=-1, keepdims=True), sink)
    p = jnp.exp(sc - m)
    den = p.sum(axis=-1, keepdims=True) + jnp.exp(sink - m)
    return jnp.einsum('nhgij,njhd->nihgd', (p / den).astype(v.dtype), v)


def swa_prompt(q, k, v, sinks):
    B, S = q.shape[:2]
    n = S // CHUNK
    pad = WIN_CHUNKS * CHUNK

    def band(t):
        tp = jnp.pad(t, ((0, 0), (pad, 0), (0, 0), (0, 0))).reshape(B, n + WIN_CHUNKS, CHUNK, SWA_KV_HEADS, SWA_HD)
        kb = jnp.concatenate([tp[:, w:w + n] for w in range(WIN_CHUNKS + 1)], axis=2)
        return kb.reshape(B * n, (WIN_CHUNKS + 1) * CHUNK, SWA_KV_HEADS, SWA_HD)

    blk = jnp.repeat(jnp.arange(WIN_CHUNKS + 1), CHUNK)
    valid = (jnp.arange(n)[:, None] + blk[None, :] - WIN_CHUNKS) >= 0
    valid = jnp.tile(valid, (B, 1))
    qb = q.reshape(B * n, CHUNK, SWA_KV_HEADS, SWA_GROUP, SWA_HD)
    o = sink_attention(qb, band(k), band(v), sinks, valid)
    return o.reshape(B, S, SWA_HEADS * SWA_HD)


def layer(x, pe, s0, k_cache, v_cache, g_ffn1, w1_up, w1_down, g_mix, w_in, w_a2, b_a, g_gla,
          sinks, w_out, g_ffn2, w2_up, w2_down, g_ple, w_ple_gate, w_ple):
    B, L, _ = x.shape
    f32 = jnp.float32
    h = x + 0.5 * swiglu(rmsnorm(x, g_ffn1), w1_up, w1_down)
    u = rmsnorm(h, g_mix)
    idx = np.cumsum(SPLITS)[:-1].tolist()
    ga, gb, qa, ka, va, ra, alr, qb, kb, vb = jnp.split(u @ w_in, idx, axis=-1)
    log_a = jax.nn.log_sigmoid((alr @ w_a2 + b_a).astype(f32)) / GLA_TAU
    o_a, s_new = gla_scan(
        (qa.reshape(B, L, GLA_HEADS, GLA_DK) * GLA_DK ** -0.5).astype(f32),
        ka.reshape(B, L, GLA_HEADS, GLA_DK).astype(f32),
        va.reshape(B, L, GLA_HEADS, GLA_DV).astype(f32),
        log_a.reshape(B, L, GLA_HEADS, GLA_DK),
        s0.astype(f32))
    o_a = rmsnorm(o_a, g_gla.reshape(GLA_HEADS, GLA_DV)).astype(x.dtype).reshape(B, L, -1) * jax.nn.silu(ra)
    q = qb.reshape(B, L, SWA_KV_HEADS, SWA_GROUP, SWA_HD)
    k = kb.reshape(B, L, SWA_KV_HEADS, SWA_HD)
    v = vb.reshape(B, L, SWA_KV_HEADS, SWA_HD)
    if k_cache is None:
        o_b = swa_prompt(q, k, v, sinks)
        keep = min(WINDOW, L)
        k_new, v_new = k[:, -keep:], v[:, -keep:]
    else:
        kf = jnp.concatenate([k_cache.astype(k.dtype), k], axis=1)
        vf = jnp.concatenate([v_cache.astype(v.dtype), v], axis=1)
        o_b = sink_attention(q, kf, vf, sinks, None).reshape(B, L, SWA_HEADS * SWA_HD)
        keep = k_cache.shape[1]
        k_new, v_new = kf[:, -keep:], vf[:, -keep:]
    merged = jax.nn.sigmoid(ga) * o_a + jax.nn.sigmoid(gb) * o_b
    h = h + merged @ w_out
    h = h + 0.5 * swiglu(rmsnorm(h, g_ffn2), w2_up, w2_down)
    y = h + jax.nn.sigmoid(rmsnorm(h, g_ple) @ w_ple_gate) * (pe @ w_ple)
    return y, k_new, v_new, s_new.astype(s0.dtype)


def setup_inputs(seed: int = 0) -> dict:
    key = jax.random.key(seed)
    ks = jax.random.split(key, 32)
    nrm = jax.random.normal
    D = D_MODEL
    win = min(WINDOW, PAST_LEN)

    def gain(k, width):
        return 1.0 + 0.02 * nrm(k, (DEPTH, width), jnp.float32)

    return {
        'x_prompt': nrm(ks[0], (BATCH, SEQ, D), jnp.float32),
        'x_sample': nrm(ks[1], (DEC_BATCH, DEC_SEQ, D), jnp.float32),
        'p_prompt': nrm(ks[2], (DEPTH, BATCH, SEQ, PLE_DIM), jnp.float32),
        'p_sample': nrm(ks[3], (DEPTH, DEC_BATCH, DEC_SEQ, PLE_DIM), jnp.float32),
        'cache_swa_k': nrm(ks[4], (DEPTH, DEC_BATCH, win, SWA_KV_HEADS, SWA_HD), jnp.float32),
        'cache_swa_v': nrm(ks[5], (DEPTH, DEC_BATCH, win, SWA_KV_HEADS, SWA_HD), jnp.float32),
        'state_gla': 0.5 * nrm(ks[6], (DEPTH, DEC_BATCH, GLA_HEADS, GLA_DK, GLA_DV), jnp.float32),
        'g_ffn1': gain(ks[7], D),
        'w_ffn1_up': nrm(ks[8], (DEPTH, D, 2 * D_FF), jnp.float32) * D ** -0.5,
        'w_ffn1_down': nrm(ks[9], (DEPTH, D_FF, D), jnp.float32) * D_FF ** -0.5,
        'g_mix': gain(ks[10], D),
        'w_in': nrm(ks[11], (DEPTH, D, D_IN), jnp.float32) * D ** -0.5,
        'w_gla_a2': nrm(ks[12], (DEPTH, GLA_RANK, GLA_HEADS * GLA_DK), jnp.float32) * GLA_RANK ** -0.5,
        'b_gla_a': 0.01 * nrm(ks[13], (DEPTH, GLA_HEADS * GLA_DK), jnp.float32),
        'g_gla': gain(ks[14], GLA_HEADS * GLA_DV),
        'swa_sinks': 0.5 * nrm(ks[15], (DEPTH, SWA_HEADS), jnp.float32),
        'w_out': nrm(ks[16], (DEPTH, D, D), jnp.float32) * D ** -0.5,
        'g_ffn2': gain(ks[17], D),
        'w_ffn2_up': nrm(ks[18], (DEPTH, D, 2 * D_FF), jnp.float32) * D ** -0.5,
        'w_ffn2_down': nrm(ks[19], (DEPTH, D_FF, D), jnp.float32) * D_FF ** -0.5,
        'g_ple': gain(ks[20], D),
        'w_ple_gate': nrm(ks[21], (DEPTH, D, D), jnp.float32) * D ** -0.5,
        'w_ple': nrm(ks[22], (DEPTH, PLE_DIM, D), jnp.float32) * PLE_DIM ** -0.5,
        'g_final': 1.0 + 0.02 * nrm(ks[23], (D,), jnp.float32),
    }


def reference(x_prompt, x_sample, p_prompt, p_sample, cache_swa_k, cache_swa_v, state_gla,
              g_ffn1, w_ffn1_up, w_ffn1_down, g_mix, w_in, w_gla_a2, b_gla_a, g_gla, swa_sinks,
              w_out, g_ffn2, w_ffn2_up, w_ffn2_down, g_ple, w_ple_gate, w_ple, g_final):
    xp, xs = x_prompt, x_sample
    s0_prompt = jnp.zeros((x_prompt.shape[0], GLA_HEADS, GLA_DK, GLA_DV), jnp.float32)
    pk, pv, ps, sk, sv, ss = [], [], [], [], [], []
    for i in range(DEPTH):
        w = (g_ffn1[i], w_ffn1_up[i], w_ffn1_down[i], g_mix[i], w_in[i], w_gla_a2[i], b_gla_a[i],
             g_gla[i], swa_sinks[i], w_out[i], g_ffn2[i], w_ffn2_up[i], w_ffn2_down[i],
             g_ple[i], w_ple_gate[i], w_ple[i])
        xp, k1, v1, s1 = layer(xp, p_prompt[i], s0_prompt, None, None, *w)
        xs, k2, v2, s2 = layer(xs, p_sample[i], state_gla[i], cache_swa_k[i], cache_swa_v[i], *w)
        pk.append(k1); pv.append(v1); ps.append(s1)
        sk.append(k2); sv.append(v2); ss.append(s2)
    y_prompt = rmsnorm(xp, g_final)
    y_sample = rmsnorm(xs, g_final)
    return (y_prompt, y_sample, jnp.stack(pk), jnp.stack(pv), jnp.stack(ps),
            jnp.stack(sk), jnp.stack(sv), jnp.stack(ss))
```

```python
import functools

import jax
import jax.numpy as jnp
from jax import lax
from jax.experimental import pallas as pl
from jax.experimental.pallas import tpu as pltpu

F32 = jnp.float32
BF16 = jnp.bfloat16

CHUNK = 64
EPS = 1e-6
GLA_HEADS = 4
GLA_RANK = 16
GLA_TAU = 16.0
SWA_HEADS = 16
SWA_KV_HEADS = 4
SWA_GROUP = SWA_HEADS // SWA_KV_HEADS
WINDOW = 128
BAND = WINDOW + CHUNK

LANES = 128
MXU_N = 256
VMEM_LIMIT_BYTES = 56 * 1024 * 1024

GLA_SAFE_LOG_DECAY = -60.0


def _sigmoid(x):
    return 1.0 / (1.0 + jnp.exp(-x))


def _log_sigmoid(x):
    return jnp.minimum(x, 0.0) - jnp.log(1.0 + jnp.exp(-jnp.abs(x)))


def _rms_rows(x, g):
    ms = jnp.mean(x * x, axis=-1, keepdims=True)
    return x * lax.rsqrt(ms + EPS) * g


def _dot(a, b):
    return jnp.dot(a, b, preferred_element_type=F32)


def _dot_nt(a, b):
    return lax.dot_general(a, b, (((1,), (1,)), ((), ())), preferred_element_type=F32)


def _norm_rows_to(src_ref, g_ref, dst_ref, rows, rb):
    g = g_ref[...]

    def body(i, carry):
        r = pl.multiple_of(i * rb, rb)
        dst_ref[pl.ds(r, rb), :] = _rms_rows(src_ref[pl.ds(r, rb), :], g).astype(dst_ref.dtype)
        return carry

    lax.fori_loop(0, rows // rb, body, 0)


def _ffn_kernel(*refs, tm, d_ff, has_ple, has_final):
    it = iter(refs)
    x_ref, g_ref, wup_ref, wdn_ref = next(it), next(it), next(it), next(it)
    if has_ple:
        pe_ref, gple_ref, wpg_ref, wpl_ref = next(it), next(it), next(it), next(it)
    if has_final:
        gfin_ref = next(it)
    o_ref = next(it)
    xn_ref, act_ref = next(it), next(it)
    if has_ple:
        h_ref = next(it)
    d = x_ref.shape[1]
    rb = min(tm, 64)

    _norm_rows_to(x_ref, g_ref, xn_ref, tm, rb)

    def up_body(c, carry):
        c0 = pl.multiple_of(c * MXU_N, MXU_N)
        c1 = pl.multiple_of(d_ff + c * MXU_N, MXU_N)
        xn = xn_ref[...]
        gate = _dot(xn, wup_ref[:, pl.ds(c0, MXU_N)])
        up = _dot(xn, wup_ref[:, pl.ds(c1, MXU_N)])
        act_ref[:, pl.ds(c0, MXU_N)] = (gate * _sigmoid(gate) * up).astype(BF16)
        return carry

    lax.fori_loop(0, d_ff // MXU_N, up_body, 0)

    dst_ref = h_ref if has_ple else o_ref

    def down_body(n, carry):
        n0 = pl.multiple_of(n * MXU_N, MXU_N)
        y = _dot(act_ref[...], wdn_ref[:, pl.ds(n0, MXU_N)])
        dst_ref[:, pl.ds(n0, MXU_N)] = x_ref[:, pl.ds(n0, MXU_N)] + 0.5 * y
        return carry

    lax.fori_loop(0, d // MXU_N, down_body, 0)

    if has_ple:
        _norm_rows_to(h_ref, gple_ref, xn_ref, tm, rb)

        def ple_body(n, carry):
            n0 = pl.multiple_of(n * MXU_N, MXU_N)
            gate = _sigmoid(_dot(xn_ref[...], wpg_ref[:, pl.ds(n0, MXU_N)]))
            emb = _dot(pe_ref[...].astype(BF16), wpl_ref[:, pl.ds(n0, MXU_N)])
            o_ref[:, pl.ds(n0, MXU_N)] = h_ref[:, pl.ds(n0, MXU_N)] + gate * emb
            return carry

        lax.fori_loop(0, d // MXU_N, ple_body, 0)

    if has_final:
        _norm_rows_to(o_ref, gfin_ref, o_ref, tm, rb)


def _const_spec(shape):
    zeros = (0,) * len(shape)
    return pl.BlockSpec(shape, lambda *_: zeros, pipeline_mode=pl.Buffered(1))


def _ffn_call(x, g, wup, wdn, ple=None, g_final=None):
    n, d = x.shape
    d_ff = wdn.shape[0]
    tm = min(n, 512)
    assert n % tm == 0 and d_ff % MXU_N == 0 and d % MXU_N == 0
    has_ple, has_final = ple is not None, g_final is not None
    row_spec = lambda w: pl.BlockSpec((tm, w), lambda i: (i, 0))
    args = [x, g.reshape(1, d), wup, wdn]
    specs = [row_spec(d), _const_spec((1, d)), _const_spec(wup.shape), _const_spec(wdn.shape)]
    scratch = [pltpu.VMEM((tm, d), BF16), pltpu.VMEM((tm, d_ff), BF16)]
    if has_ple:
        pe, gple, wpg, wpl = ple
        args += [pe, gple.reshape(1, d), wpg, wpl]
        specs += [row_spec(pe.shape[1]), _const_spec((1, d)), _const_spec(wpg.shape), _const_spec(wpl.shape)]
        scratch.append(pltpu.VMEM((tm, d), F32))
    if has_final:
        args.append(g_final.reshape(1, d))
        specs.append(_const_spec((1, d)))
    return pl.pallas_call(
        functools.partial(_ffn_kernel, tm=tm, d_ff=d_ff, has_ple=has_ple, has_final=has_final),
        out_shape=jax.ShapeDtypeStruct((n, d), F32),
        grid=(n // tm,),
        in_specs=specs,
        out_specs=row_spec(d),
        scratch_shapes=scratch,
        compiler_params=pltpu.CompilerParams(
            dimension_semantics=("arbitrary",), vmem_limit_bytes=VMEM_LIMIT_BYTES),
        name="ffn_ple" if has_ple else "ffn",
    )(*args)


def _mixer_kernel(*refs, t_tile, d, dk, dv, hd, has_cache):
    it = iter(refs)
    h_ref = next(it)
    if has_cache:
        ck_ref, cv_ref, s0_ref = next(it), next(it), next(it)
    gmix_ref, win_ref, walr_ref, wa2_ref, ba_ref, ggla_ref, sink_ref, wout_ref = (next(it) for _ in range(8))
    o_ref, knew_ref, vnew_ref, sfin_ref = next(it), next(it), next(it), next(it)
    un_ref, z_ref, la_ref, mg_ref, kbuf_ref, vbuf_ref, s_ref, sc_ref, b_ref = (next(it) for _ in range(9))

    t = pl.program_id(1)
    n_t = pl.num_programs(1)
    gk = GLA_HEADS * dk
    gv = GLA_HEADS * dv
    kvw = SWA_KV_HEADS * hd
    c_ga, c_gb, c_ra, c_va = 0, d, 2 * d, 3 * d
    c_qa = 4 * d
    c_ka = c_qa + gk
    c_qb = c_ka + gk
    c_kb = c_qb + d
    c_vb = c_kb + kvw
    z_w = c_vb + kvw

    @pl.when(t == 0)
    def _init():
        if has_cache:
            s_ref[...] = s0_ref[0]
            kbuf_ref[0:WINDOW, :] = ck_ref[0]
            vbuf_ref[0:WINDOW, :] = cv_ref[0]
        else:
            s_ref[...] = jnp.zeros(s_ref.shape, F32)
            kbuf_ref[0:WINDOW, :] = jnp.zeros((WINDOW, kvw), F32)
            vbuf_ref[0:WINDOW, :] = jnp.zeros((WINDOW, kvw), F32)

    rb = min(t_tile, 64)
    g_mix = gmix_ref[...]

    def norm_body(i, carry):
        r = pl.multiple_of(i * rb, rb)
        un_ref[pl.ds(r, rb), :] = _rms_rows(h_ref[0, pl.ds(r, rb), :], g_mix).astype(BF16)
        return carry

    lax.fori_loop(0, t_tile // rb, norm_body, 0)

    zc = 2 * MXU_N

    def z_body(c, carry):
        c0 = pl.multiple_of(c * zc, zc)
        z_ref[:, pl.ds(c0, zc)] = _dot(un_ref[...], win_ref[:, pl.ds(c0, zc)])
        return carry

    lax.fori_loop(0, z_w // zc, z_body, 0)

    alr = _dot(un_ref[...], walr_ref[...]).astype(BF16)
    la_ref[...] = _log_sigmoid(_dot(alr, wa2_ref[...]) + ba_ref[...]) * (1.0 / GLA_TAU)
    kbuf_ref[WINDOW:WINDOW + t_tile, :] = z_ref[:, c_kb:c_kb + kvw]
    vbuf_ref[WINDOW:WINDOW + t_tile, :] = z_ref[:, c_vb:c_vb + kvw]

    row = lax.broadcasted_iota(jnp.int32, (CHUNK, CHUNK), 0)
    col = lax.broadcasted_iota(jnp.int32, (CHUNK, CHUNK), 1)
    causal = row >= col
    tri = jnp.where(causal, 1.0, 0.0).astype(BF16)
    g_gla = ggla_ref[...]
    n_chunks = t_tile // CHUNK

    def chunk_body(c, carry):
        r0 = pl.multiple_of(c * CHUNK, CHUNK)
        rows = pl.ds(r0, CHUNK)

        la = la_ref[rows, :]
        la_hi = la.astype(BF16)
        la_lo = (la - la_hi.astype(F32)).astype(BF16)
        b = _dot(tri, la_hi) + _dot(tri, la_lo)
        b_last = b[CHUNK - 1:CHUNK, :]
        q = z_ref[rows, c_qa:c_qa + gk] * (dk ** -0.5)
        k = z_ref[rows, c_ka:c_ka + gk]
        qg = q * jnp.exp(b)
        kdec = k * jnp.exp(b_last - b)
        a_last = jnp.exp(b_last)
        safe = jnp.min(b_last) >= GLA_SAFE_LOG_DECAY

        @pl.when(safe)
        def _scores_matmul():
            kt = (k * jnp.exp(-b)).astype(BF16)
            qgb = qg.astype(BF16)
            for hh in range(GLA_HEADS):
                s = _dot_nt(qgb[:, hh * dk:(hh + 1) * dk], kt[:, hh * dk:(hh + 1) * dk])
                sc_ref[:, hh * CHUNK:(hh + 1) * CHUNK] = jnp.where(causal, s, 0.0)

        @pl.when(jnp.logical_not(safe))
        def _scores_exact():
            b_ref[...] = b

            def key_body(j, accs):
                bj = b_ref[pl.ds(j, 1), :]
                kj = z_ref[pl.ds(r0 + j, 1), c_ka:c_ka + gk]
                w = q * jnp.exp(jnp.minimum(b - bj, 0.0)) * kj
                out = []
                for hh in range(GLA_HEADS):
                    colsum = jnp.sum(w[:, hh * dk:(hh + 1) * dk], axis=1, keepdims=True)
                    out.append(jnp.where((col == j) & causal, colsum, accs[hh]))
                return tuple(out)

            accs = lax.fori_loop(0, CHUNK, key_body,
                                 tuple(jnp.zeros((CHUNK, CHUNK), F32) for _ in range(GLA_HEADS)))
            for hh in range(GLA_HEADS):
                sc_ref[:, hh * CHUNK:(hh + 1) * CHUNK] = accs[hh]

        pad = jnp.zeros((LANES - CHUNK - 8, dk), F32)
        merged = []
        for hh in range(GLA_HEADS):
            v = z_ref[rows, c_va + hh * dv:c_va + (hh + 1) * dv].astype(BF16)
            s_old = s_ref[hh]
            sc = sc_ref[:, hh * CHUNK:(hh + 1) * CHUNK].astype(BF16)
            o = _dot(sc, v) + _dot(qg[:, hh * dk:(hh + 1) * dk].astype(BF16), s_old.astype(BF16))
            stacked = jnp.concatenate(
                [kdec[:, hh * dk:(hh + 1) * dk],
                 jnp.broadcast_to(a_last[:, hh * dk:(hh + 1) * dk], (8, dk)), pad], axis=0)
            st = stacked.T
            s_ref[hh] = s_old * st[:, CHUNK:CHUNK + 1] + _dot(st[:, 0:CHUNK].astype(BF16), v)
            o = _rms_rows(o, g_gla[:, hh * dv:(hh + 1) * dv])
            ra = z_ref[rows, c_ra + hh * dv:c_ra + (hh + 1) * dv]
            ga = z_ref[rows, c_ga + hh * dv:c_ga + (hh + 1) * dv]
            merged.append(_sigmoid(ga) * (o * (ra * _sigmoid(ra))))

        band = pl.ds(r0, BAND)
        if not has_cache:
            cg = t * n_chunks + c
            kcol = lax.broadcasted_iota(jnp.int32, (CHUNK, BAND), 1)
            valid = kcol >= WINDOW - CHUNK * jnp.minimum(cg, WINDOW // CHUNK)
        gw = SWA_GROUP * hd
        for g in range(SWA_KV_HEADS):
            kb = kbuf_ref[band, g * hd:(g + 1) * hd].astype(BF16)
            vb = vbuf_ref[band, g * hd:(g + 1) * hd].astype(BF16)
            qgrp = z_ref[rows, c_qb + g * gw:c_qb + (g + 1) * gw] * (hd ** -0.5)
            qs = jnp.concatenate([qgrp[:, m * hd:(m + 1) * hd] for m in range(SWA_GROUP)], axis=0)
            s = _dot_nt(qs.astype(BF16), kb)
            ps = []
            for m in range(SWA_GROUP):
                sm = s[m * CHUNK:(m + 1) * CHUNK, :]
                if not has_cache:
                    sm = jnp.where(valid, sm, -jnp.inf)
                sink = sink_ref[g * SWA_GROUP + m]
                mx = jnp.maximum(jnp.max(sm, axis=1, keepdims=True), sink)
                p = jnp.exp(sm - mx)
                den = jnp.sum(p, axis=1, keepdims=True) + jnp.exp(sink - mx)
                ps.append((p / den).astype(BF16))
            og = _dot(jnp.concatenate(ps, axis=0), vb)
            ob = jnp.concatenate([og[m * CHUNK:(m + 1) * CHUNK, :] for m in range(SWA_GROUP)], axis=1)
            gb = z_ref[rows, c_gb + g * gw:c_gb + (g + 1) * gw]
            mg_ref[rows, g * gw:(g + 1) * gw] = (merged[g] + _sigmoid(gb) * ob).astype(BF16)
        return carry

    lax.fori_loop(0, n_chunks, chunk_body, 0)

    def out_body(n, carry):
        n0 = pl.multiple_of(n * MXU_N, MXU_N)
        o_ref[0, :, pl.ds(n0, MXU_N)] = h_ref[0, :, pl.ds(n0, MXU_N)] + _dot(mg_ref[...], wout_ref[:, pl.ds(n0, MXU_N)])
        return carry

    lax.fori_loop(0, d // MXU_N, out_body, 0)

    k_tail = kbuf_ref[t_tile:t_tile + WINDOW, :]
    v_tail = vbuf_ref[t_tile:t_tile + WINDOW, :]
    kbuf_ref[0:WINDOW, :] = k_tail
    vbuf_ref[0:WINDOW, :] = v_tail

    @pl.when(t == n_t - 1)
    def _emit():
        knew_ref[0] = k_tail
        vnew_ref[0] = v_tail
        sfin_ref[0] = s_ref[...]


def _mixer_call(h, cache, gmix, win, walr, wa2, ba, ggla, sinks, wout, *, dk, dv, hd):
    bsz, seq, d = h.shape
    t_tile = min(seq, 256)
    assert seq % t_tile == 0 and t_tile % CHUNK == 0
    has_cache = cache is not None
    kvw = SWA_KV_HEADS * hd
    gk = GLA_HEADS * dk
    z_w = win.shape[1]
    assert z_w % (2 * MXU_N) == 0 and GLA_HEADS * dv == d and SWA_HEADS * hd == d and SWA_GROUP * hd == dv

    seq_spec = pl.BlockSpec((1, t_tile, d), lambda b, t: (b, t, 0))
    per_b = lambda *shape: pl.BlockSpec((1,) + shape, lambda b, t: (b,) + (0,) * len(shape))
    args, specs = [h], [seq_spec]
    if has_cache:
        ck, cv, s0 = cache
        args += [ck, cv, s0]
        specs += [per_b(WINDOW, kvw), per_b(WINDOW, kvw), per_b(GLA_HEADS, dk, dv)]
    args += [gmix.reshape(1, d), win, walr, wa2, ba.reshape(1, gk), ggla.reshape(1, d), sinks, wout]
    specs += [_const_spec((1, d)), _const_spec(win.shape), _const_spec(walr.shape), _const_spec(wa2.shape),
              _const_spec((1, gk)), _const_spec((1, d)), pl.BlockSpec(memory_space=pltpu.SMEM),
              _const_spec(wout.shape)]
    out_shape = (jax.ShapeDtypeStruct((bsz, seq, d), F32),
                 jax.ShapeDtypeStruct((bsz, WINDOW, kvw), F32),
                 jax.ShapeDtypeStruct((bsz, WINDOW, kvw), F32),
                 jax.ShapeDtypeStruct((bsz, GLA_HEADS, dk, dv), F32))
    out_specs = (seq_spec, per_b(WINDOW, kvw), per_b(WINDOW, kvw), per_b(GLA_HEADS, dk, dv))
    scratch = [
        pltpu.VMEM((t_tile, d), BF16),
        pltpu.VMEM((t_tile, z_w), F32),
        pltpu.VMEM((t_tile, gk), F32),
        pltpu.VMEM((t_tile, d), BF16),
        pltpu.VMEM((WINDOW + t_tile, kvw), F32),
        pltpu.VMEM((WINDOW + t_tile, kvw), F32),
        pltpu.VMEM((GLA_HEADS, dk, dv), F32),
        pltpu.VMEM((CHUNK, GLA_HEADS * CHUNK), F32),
        pltpu.VMEM((CHUNK, gk), F32),
    ]
    return pl.pallas_call(
        functools.partial(_mixer_kernel, t_tile=t_tile, d=d, dk=dk, dv=dv, hd=hd, has_cache=has_cache),
        out_shape=out_shape,
        grid=(bsz, seq // t_tile),
        in_specs=specs,
        out_specs=out_specs,
        scratch_shapes=scratch,
        compiler_params=pltpu.CompilerParams(
            dimension_semantics=("arbitrary", "arbitrary"), vmem_limit_bytes=VMEM_LIMIT_BYTES),
        name="mixer_cache" if has_cache else "mixer",
    )(*args)


def _prep_layer(i, d, dk, dv, hd, w_ffn1_up, w_ffn1_down, w_in, w_gla_a2, w_out, w_ffn2_up, w_ffn2_down,
                w_ple_gate, w_ple):
    gk, kvw = GLA_HEADS * dk, SWA_KV_HEADS * hd
    widths = (d, d, gk, gk, d, d, GLA_RANK, d, kvw, kvw)
    offs = [0]
    for w in widths:
        offs.append(offs[-1] + w)
    part = lambda j: w_in[i][:, offs[j]:offs[j + 1]]
    order = (0, 1, 5, 4, 2, 3, 7, 8, 9)
    win = jnp.concatenate([part(j) for j in order], axis=1).astype(BF16)
    walr = jnp.pad(part(6), ((0, 0), (0, LANES - GLA_RANK))).astype(BF16)
    wa2 = jnp.pad(w_gla_a2[i], ((0, LANES - GLA_RANK), (0, 0))).astype(BF16)
    return dict(
        up1=w_ffn1_up[i].astype(BF16), dn1=w_ffn1_down[i].astype(BF16),
        up2=w_ffn2_up[i].astype(BF16), dn2=w_ffn2_down[i].astype(BF16),
        win=win, walr=walr, wa2=wa2, wout=w_out[i].astype(BF16),
        wpg=w_ple_gate[i].astype(BF16), wpl=w_ple[i].astype(BF16))


def kernel(x_prompt, x_sample, p_prompt, p_sample, cache_swa_k, cache_swa_v, state_gla, g_ffn1, w_ffn1_up, w_ffn1_down, g_mix, w_in, w_gla_a2, b_gla_a, g_gla, swa_sinks, w_out, g_ffn2, w_ffn2_up, w_ffn2_down, g_ple, w_ple_gate, w_ple, g_final):
    depth = w_in.shape[0]
    bp, sp, d = x_prompt.shape
    bs, ss, _ = x_sample.shape
    dk = state_gla.shape[-2]
    dv = state_gla.shape[-1]
    hd = cache_swa_k.shape[-1]
    kvw = SWA_KV_HEADS * hd
    wc = cache_swa_k.shape[2]
    assert wc == WINDOW and sp >= WINDOW

    xp = x_prompt.reshape(bp * sp, d)
    xs = x_sample.reshape(bs * ss, d)
    outs = [[] for _ in range(6)]
    for i in range(depth):
        w = _prep_layer(i, d, dk, dv, hd, w_ffn1_up, w_ffn1_down, w_in, w_gla_a2, w_out, w_ffn2_up,
                        w_ffn2_down, w_ple_gate, w_ple)
        last = g_final if i == depth - 1 else None
        mix = functools.partial(_mixer_call, gmix=g_mix[i], win=w["win"], walr=w["walr"], wa2=w["wa2"],
                                ba=b_gla_a[i], ggla=g_gla[i], sinks=swa_sinks[i], wout=w["wout"],
                                dk=dk, dv=dv, hd=hd)
        xp = _ffn_call(xp, g_ffn1[i], w["up1"], w["dn1"])
        xp3, pk, pv, ps = mix(xp.reshape(bp, sp, d), None)
        xp = _ffn_call(xp3.reshape(bp * sp, d), g_ffn2[i], w["up2"], w["dn2"],
                       ple=(p_prompt[i].reshape(bp * sp, -1), g_ple[i], w["wpg"], w["wpl"]), g_final=last)
        xs = _ffn_call(xs, g_ffn1[i], w["up1"], w["dn1"])
        cache = (cache_swa_k[i].reshape(bs, wc, kvw), cache_swa_v[i].reshape(bs, wc, kvw), state_gla[i])
        xs3, sk, sv, s_s = mix(xs.reshape(bs, ss, d), cache)
        xs = _ffn_call(xs3.reshape(bs * ss, d), g_ffn2[i], w["up2"], w["dn2"],
                       ple=(p_sample[i].reshape(bs * ss, -1), g_ple[i], w["wpg"], w["wpl"]), g_final=last)
        for lst, val in zip(outs, (pk.reshape(bp, WINDOW, SWA_KV_HEADS, hd), pv.reshape(bp, WINDOW, SWA_KV_HEADS, hd),
                                   ps, sk.reshape(bs, wc, SWA_KV_HEADS, hd), sv.reshape(bs, wc, SWA_KV_HEADS, hd), s_s)):
            lst.append(val)
    return (xp.reshape(bp, sp, d), xs.reshape(bs, ss, d)) + tuple(jnp.stack(o) for o in outs)
```

```python
import functools

import jax
import jax.numpy as jnp
from jax import lax
from jax.experimental import pallas as pl
from jax.experimental.pallas import tpu as pltpu

F32 = jnp.float32
BF16 = jnp.bfloat16

CHUNK = 64
EPS = 1e-6
GLA_HEADS = 4
GLA_RANK = 16
GLA_TAU = 16.0
SWA_HEADS = 16
SWA_KV_HEADS = 4
SWA_GROUP = SWA_HEADS // SWA_KV_HEADS
WINDOW = 128
BAND = WINDOW + CHUNK

LANES = 128
MXU_N = 256
VMEM_LIMIT_BYTES = 56 * 1024 * 1024

GLA_SAFE_LOG_DECAY = -60.0


def _sigmoid(x):
    return 1.0 / (1.0 + jnp.exp(-x))


def _log_sigmoid(x):
    return jnp.minimum(x, 0.0) - jnp.log(1.0 + jnp.exp(-jnp.abs(x)))


def _rms_rows(x, g):
    ms = jnp.mean(x * x, axis=-1, keepdims=True)
    return x * lax.rsqrt(ms + EPS) * g


def _dot(a, b):
    return jnp.dot(a, b, preferred_element_type=F32)


def _dot_nt(a, b):
    return lax.dot_general(a, b, (((1,), (1,)), ((), ())), preferred_element_type=F32)


def _norm_rows_to(src_ref, g_ref, dst_ref, rows, rb):
    g = g_ref[...]
    for r in range(0, rows, rb):
        dst_ref[r:r + rb, :] = _rms_rows(src_ref[r:r + rb, :], g).astype(dst_ref.dtype)


def _ffn_kernel(*refs, tm, d_ff, has_ple, has_final):
    it = iter(refs)
    x_ref, g_ref, wup_ref, wdn_ref = next(it), next(it), next(it), next(it)
    if has_ple:
        pe_ref, gple_ref, wpg_ref, wpl_ref = next(it), next(it), next(it), next(it)
    if has_final:
        gfin_ref = next(it)
    o_ref = next(it)
    xn_ref, act_ref = next(it), next(it)
    if has_ple:
        h_ref = next(it)
    d = x_ref.shape[1]
    rb = min(tm, 64)

    _norm_rows_to(x_ref, g_ref, xn_ref, tm, rb)

    for c0 in range(0, d_ff, MXU_N):
        xn = xn_ref[...]
        gate = _dot(xn, wup_ref[:, c0:c0 + MXU_N])
        up = _dot(xn, wup_ref[:, d_ff + c0:d_ff + c0 + MXU_N])
        act_ref[:, c0:c0 + MXU_N] = (gate * _sigmoid(gate) * up).astype(BF16)

    dst_ref = h_ref if has_ple else o_ref
    for n0 in range(0, d, MXU_N):
        y = _dot(act_ref[...], wdn_ref[:, n0:n0 + MXU_N])
        dst_ref[:, n0:n0 + MXU_N] = x_ref[:, n0:n0 + MXU_N] + 0.5 * y

    if has_ple:
        _norm_rows_to(h_ref, gple_ref, xn_ref, tm, rb)
        for n0 in range(0, d, MXU_N):
            gate = _sigmoid(_dot(xn_ref[...], wpg_ref[:, n0:n0 + MXU_N]))
            emb = _dot(pe_ref[...].astype(BF16), wpl_ref[:, n0:n0 + MXU_N])
            o_ref[:, n0:n0 + MXU_N] = h_ref[:, n0:n0 + MXU_N] + gate * emb

    if has_final:
        _norm_rows_to(o_ref, gfin_ref, o_ref, tm, rb)


def _const_spec(shape):
    zeros = (0,) * len(shape)
    return pl.BlockSpec(shape, lambda *_: zeros, pipeline_mode=pl.Buffered(1))


def _ffn_call(x, g, wup, wdn, ple=None, g_final=None):
    n, d = x.shape
    d_ff = wdn.shape[0]
    tm = min(n, 512)
    assert n % tm == 0 and d_ff % MXU_N == 0 and d % MXU_N == 0
    has_ple, has_final = ple is not None, g_final is not None
    row_spec = lambda w: pl.BlockSpec((tm, w), lambda i: (i, 0))
    args = [x, g.reshape(1, d), wup, wdn]
    specs = [row_spec(d), _const_spec((1, d)), _const_spec(wup.shape), _const_spec(wdn.shape)]
    scratch = [pltpu.VMEM((tm, d), BF16), pltpu.VMEM((tm, d_ff), BF16)]
    if has_ple:
        pe, gple, wpg, wpl = ple
        args += [pe, gple.reshape(1, d), wpg, wpl]
        specs += [row_spec(pe.shape[1]), _const_spec((1, d)), _const_spec(wpg.shape), _const_spec(wpl.shape)]
        scratch.append(pltpu.VMEM((tm, d), F32))
    if has_final:
        args.append(g_final.reshape(1, d))
        specs.append(_const_spec((1, d)))
    return pl.pallas_call(
        functools.partial(_ffn_kernel, tm=tm, d_ff=d_ff, has_ple=has_ple, has_final=has_final),
        out_shape=jax.ShapeDtypeStruct((n, d), F32),
        grid=(n // tm,),
        in_specs=specs,
        out_specs=row_spec(d),
        scratch_shapes=scratch,
        compiler_params=pltpu.CompilerParams(
            dimension_semantics=("arbitrary",), vmem_limit_bytes=VMEM_LIMIT_BYTES),
        name="ffn_ple" if has_ple else "ffn",
    )(*args)


def _mixer_kernel(*refs, t_tile, n_t, d, dk, dv, hd, has_cache):
    it = iter(refs)
    h_ref = next(it)
    if has_cache:
        ck_ref, cv_ref, s0_ref = next(it), next(it), next(it)
    gmix_ref, win_ref, walr_ref, wa2_ref, ba_ref, ggla_ref, sink_ref, wout_ref = (next(it) for _ in range(8))
    o_ref, knew_ref, vnew_ref, sfin_ref = next(it), next(it), next(it), next(it)
    un_ref = next(it)
    z_refs, b_refs, h_refs = (next(it), next(it)), (next(it), next(it)), (next(it), next(it))
    mg_ref, kbuf_ref, vbuf_ref, s_ref, flag_ref = (next(it) for _ in range(5))

    g = pl.program_id(0)
    p = jnp.maximum(g - 1, 0)
    t_mix = lax.rem(p, n_t)
    gk = GLA_HEADS * dk
    kvw = SWA_KV_HEADS * hd
    gw = SWA_GROUP * hd
    n_chunks = t_tile // CHUNK
    c_ga, c_gb, c_ra, c_va = 0, d, 2 * d, 3 * d
    c_qa = 4 * d
    c_ka = c_qa + gk
    c_qb = c_ka + gk
    c_kb = c_qb + d
    c_vb = c_kb + kvw
    z_w = c_vb + kvw
    zc = 2 * MXU_N
    rb = min(t_tile, 64)

    row = lax.broadcasted_iota(jnp.int32, (CHUNK, CHUNK), 0)
    col = lax.broadcasted_iota(jnp.int32, (CHUNK, CHUNK), 1)
    causal = row >= col
    tri = jnp.where(causal, 1.0, 0.0).astype(BF16)

    @pl.when(g == 0)
    def _first():
        z_refs[1][...] = jnp.zeros(z_refs[1].shape, F32)
        b_refs[1][...] = jnp.zeros(b_refs[1].shape, F32)
        h_refs[1][...] = jnp.zeros(h_refs[1].shape, F32)
        flag_ref[1] = 1

    @pl.when(t_mix == 0)
    def _init():
        if has_cache:
            s_ref[...] = s0_ref[0]
            kbuf_ref[0:WINDOW, :] = ck_ref[0]
            vbuf_ref[0:WINDOW, :] = cv_ref[0]
        else:
            s_ref[...] = jnp.zeros(s_ref.shape, F32)
            kbuf_ref[0:WINDOW, :] = jnp.zeros((WINDOW, kvw), F32)
            vbuf_ref[0:WINDOW, :] = jnp.zeros((WINDOW, kvw), F32)

    def proj_pieces(z_dst, b_dst, h_dst, flag_idx):
        def norm():
            g_mix = gmix_ref[...]
            for r in range(0, t_tile, rb):
                x = h_ref[0, r:r + rb, :]
                h_dst[r:r + rb, :] = x
                un_ref[r:r + rb, :] = _rms_rows(x, g_mix).astype(BF16)

        def zproj(c0):
            def run():
                z_dst[:, c0:c0 + zc] = _dot(un_ref[...], win_ref[:, c0:c0 + zc])
            return run

        def decay():
            alr = _dot(un_ref[...], walr_ref[...]).astype(BF16)
            la = _log_sigmoid(_dot(alr, wa2_ref[...]) + ba_ref[...]) * (1.0 / GLA_TAU)
            lows = []
            for r in range(0, t_tile, CHUNK):
                la_c = la[r:r + CHUNK, :]
                hi = la_c.astype(BF16)
                lo = (la_c - hi.astype(F32)).astype(BF16)
                b = _dot(tri, hi) + _dot(tri, lo)
                b_dst[r:r + CHUNK, :] = b
                lows.append(b[CHUNK - 1:CHUNK, :])
            low = functools.reduce(jnp.minimum, lows)
            flag_ref[flag_idx] = (jnp.min(low) >= GLA_SAFE_LOG_DECAY).astype(jnp.int32)

        return [norm] + [zproj(c0) for c0 in range(0, z_w, zc)] + [decay]

    def mix_chunk(z_src, b_src, r0, c, exact, masked):
        rows = pl.ds(r0, CHUNK)
        g_gla = ggla_ref[...]
        b = b_src[rows, :]
        b_last = b[CHUNK - 1:CHUNK, :]
        q = z_src[rows, c_qa:c_qa + gk] * (dk ** -0.5)
        k = z_src[rows, c_ka:c_ka + gk]
        qg = q * jnp.exp(b)
        kdec = k * jnp.exp(b_last - b)
        a_last = jnp.exp(b_last)
        if exact:
            def key_body(j, accs):
                bj = b_src[pl.ds(r0 + j, 1), :]
                kj = z_src[pl.ds(r0 + j, 1), c_ka:c_ka + gk]
                w = q * jnp.exp(jnp.minimum(b - bj, 0.0)) * kj
                out = []
                for hh in range(GLA_HEADS):
                    colsum = jnp.sum(w[:, hh * dk:(hh + 1) * dk], axis=1, keepdims=True)
                    out.append(jnp.where((col == j) & causal, colsum, accs[hh]))
                return tuple(out)

            scores = lax.fori_loop(0, CHUNK, key_body,
                                   tuple(jnp.zeros((CHUNK, CHUNK), F32) for _ in range(GLA_HEADS)))
        else:
            kt = (k * jnp.exp(-b)).astype(BF16)
            qgb = qg.astype(BF16)
            scores = [jnp.where(causal, _dot_nt(qgb[:, hh * dk:(hh + 1) * dk], kt[:, hh * dk:(hh + 1) * dk]), 0.0)
                      for hh in range(GLA_HEADS)]

        pad = jnp.zeros((LANES - CHUNK - 8, dk), F32)
        merged = []
        for hh in range(GLA_HEADS):
            v = z_src[rows, c_va + hh * dv:c_va + (hh + 1) * dv].astype(BF16)
            s_old = s_ref[hh]
            o = (_dot(scores[hh].astype(BF16), v)
                 + _dot(qg[:, hh * dk:(hh + 1) * dk].astype(BF16), s_old.astype(BF16)))
            stacked = jnp.concatenate(
                [kdec[:, hh * dk:(hh + 1) * dk],
                 jnp.broadcast_to(a_last[:, hh * dk:(hh + 1) * dk], (8, dk)), pad], axis=0)
            st = stacked.T
            s_ref[hh] = s_old * st[:, CHUNK:CHUNK + 1] + _dot(st[:, 0:CHUNK].astype(BF16), v)
            o = _rms_rows(o, g_gla[:, hh * dv:(hh + 1) * dv])
            ra = z_src[rows, c_ra + hh * dv:c_ra + (hh + 1) * dv]
            ga = z_src[rows, c_ga + hh * dv:c_ga + (hh + 1) * dv]
            merged.append(_sigmoid(ga) * (o * (ra * _sigmoid(ra))))

        band = pl.ds(r0, BAND)
        if masked:
            cg = t_mix * n_chunks + c
            kcol = lax.broadcasted_iota(jnp.int32, (CHUNK, BAND), 1)
            valid = kcol >= WINDOW - CHUNK * jnp.minimum(cg, WINDOW // CHUNK)
        for grp in range(SWA_KV_HEADS):
            kb = kbuf_ref[band, grp * hd:(grp + 1) * hd].astype(BF16)
            vb = vbuf_ref[band, grp * hd:(grp + 1) * hd].astype(BF16)
            qgrp = z_src[rows, c_qb + grp * gw:c_qb + (grp + 1) * gw] * (hd ** -0.5)
            qs = jnp.concatenate([qgrp[:, m * hd:(m + 1) * hd] for m in range(SWA_GROUP)], axis=0)
            s = _dot_nt(qs.astype(BF16), kb)
            ps, dens = [], []
            for m in range(SWA_GROUP):
                sm = s[m * CHUNK:(m + 1) * CHUNK, :]
                if masked:
                    sm = jnp.where(valid, sm, -jnp.inf)
                sink = sink_ref[grp * SWA_GROUP + m]
                mx = jnp.maximum(jnp.max(sm, axis=1, keepdims=True), sink)
                pm = jnp.exp(sm - mx)
                dens.append(jnp.sum(pm, axis=1, keepdims=True) + jnp.exp(sink - mx))
                ps.append(pm.astype(BF16))
            og = _dot(jnp.concatenate(ps, axis=0), vb)
            ob = jnp.concatenate([og[m * CHUNK:(m + 1) * CHUNK, :] / dens[m] for m in range(SWA_GROUP)], axis=1)
            gb = z_src[rows, c_gb + grp * gw:c_gb + (grp + 1) * gw]
            mg_ref[rows, grp * gw:(grp + 1) * gw] = (merged[grp] + _sigmoid(gb) * ob).astype(BF16)

    def stage_kv(z_src):
        kbuf_ref[WINDOW:WINDOW + t_tile, :] = z_src[:, c_kb:c_kb + kvw]
        vbuf_ref[WINDOW:WINDOW + t_tile, :] = z_src[:, c_vb:c_vb + kvw]

    def out_proj(h_src):
        for n0 in range(0, d, MXU_N):
            o_ref[0, :, n0:n0 + MXU_N] = h_src[:, n0:n0 + MXU_N] + _dot(mg_ref[...], wout_ref[:, n0:n0 + MXU_N])

    def step(slot):
        z_dst, b_dst, h_dst = z_refs[slot], b_refs[slot], h_refs[slot]
        z_src, b_src, h_src = z_refs[1 - slot], b_refs[1 - slot], h_refs[1 - slot]
        safe = flag_ref[1 - slot] == 1

        @pl.when(safe)
        def _fused():
            pieces = proj_pieces(z_dst, b_dst, h_dst, slot)
            stage_kv(z_src)
            per_chunk = -(-len(pieces) // (n_chunks + 1))
            for c in range(n_chunks):
                for piece in pieces[c * per_chunk:(c + 1) * per_chunk]:
                    piece()
                mix_chunk(z_src, b_src, c * CHUNK, c, exact=False,
                          masked=(not has_cache) and c < WINDOW // CHUNK)
            for piece in pieces[n_chunks * per_chunk:]:
                piece()
            out_proj(h_src)

        @pl.when(jnp.logical_not(safe))
        def _exact():
            for piece in proj_pieces(z_dst, b_dst, h_dst, slot):
                piece()
            stage_kv(z_src)

            def chunk_body(c, carry):
                mix_chunk(z_src, b_src, pl.multiple_of(c * CHUNK, CHUNK), c, exact=True, masked=not has_cache)
                return carry

            lax.fori_loop(0, n_chunks, chunk_body, 0)
            out_proj(h_src)

    parity = lax.rem(g, 2)
    pl.when(parity == 0)(functools.partial(step, 0))
    pl.when(parity == 1)(functools.partial(step, 1))

    k_tail = kbuf_ref[t_tile:t_tile + WINDOW, :]
    v_tail = vbuf_ref[t_tile:t_tile + WINDOW, :]
    kbuf_ref[0:WINDOW, :] = k_tail
    vbuf_ref[0:WINDOW, :] = v_tail

    @pl.when(t_mix == n_t - 1)
    def _emit():
        knew_ref[0] = k_tail
        vnew_ref[0] = v_tail
        sfin_ref[0] = s_ref[...]


def _mixer_call(h, cache, gmix, win, walr, wa2, ba, ggla, sinks, wout, *, dk, dv, hd):
    bsz, seq, d = h.shape
    t_tile = min(seq, 256)
    assert seq % t_tile == 0 and t_tile % CHUNK == 0
    n_t = seq // t_tile
    n_tiles = bsz * n_t
    has_cache = cache is not None
    kvw = SWA_KV_HEADS * hd
    gk = GLA_HEADS * dk
    z_w = win.shape[1]
    assert z_w % (2 * MXU_N) == 0 and GLA_HEADS * dv == d and SWA_HEADS * hd == d and SWA_GROUP * hd == dv

    def proj_tile(g):
        q = jnp.minimum(g, n_tiles - 1)
        return (q // n_t, q % n_t, 0)

    def mix_tile(g):
        q = jnp.maximum(g - 1, 0)
        return (q // n_t, q % n_t, 0)

    per_seq = lambda *shape: pl.BlockSpec((1,) + shape, lambda g: (jnp.maximum(g - 1, 0) // n_t,) + (0,) * len(shape))
    args, specs = [h], [pl.BlockSpec((1, t_tile, d), proj_tile)]
    if has_cache:
        ck, cv, s0 = cache
        args += [ck, cv, s0]
        specs += [per_seq(WINDOW, kvw), per_seq(WINDOW, kvw), per_seq(GLA_HEADS, dk, dv)]
    args += [gmix.reshape(1, d), win, walr, wa2, ba.reshape(1, gk), ggla.reshape(1, d), sinks, wout]
    specs += [_const_spec((1, d)), _const_spec(win.shape), _const_spec(walr.shape), _const_spec(wa2.shape),
              _const_spec((1, gk)), _const_spec((1, d)), pl.BlockSpec(memory_space=pltpu.SMEM),
              _const_spec(wout.shape)]
    out_shape = (jax.ShapeDtypeStruct((bsz, seq, d), F32),
                 jax.ShapeDtypeStruct((bsz, WINDOW, kvw), F32),
                 jax.ShapeDtypeStruct((bsz, WINDOW, kvw), F32),
                 jax.ShapeDtypeStruct((bsz, GLA_HEADS, dk, dv), F32))
    out_specs = (pl.BlockSpec((1, t_tile, d), mix_tile), per_seq(WINDOW, kvw), per_seq(WINDOW, kvw),
                 per_seq(GLA_HEADS, dk, dv))
    scratch = [pltpu.VMEM((t_tile, d), BF16)]
    scratch += [pltpu.VMEM((t_tile, z_w), F32)] * 2
    scratch += [pltpu.VMEM((t_tile, gk), F32)] * 2
    scratch += [pltpu.VMEM((t_tile, d), F32)] * 2
    scratch += [
        pltpu.VMEM((t_tile, d), BF16),
        pltpu.VMEM((WINDOW + t_tile, kvw), F32),
        pltpu.VMEM((WINDOW + t_tile, kvw), F32),
        pltpu.VMEM((GLA_HEADS, dk, dv), F32),
        pltpu.SMEM((2,), jnp.int32),
    ]
    return pl.pallas_call(
        functools.partial(_mixer_kernel, t_tile=t_tile, n_t=n_t, d=d, dk=dk, dv=dv, hd=hd, has_cache=has_cache),
        out_shape=out_shape,
        grid=(n_tiles + 1,),
        in_specs=specs,
        out_specs=out_specs,
        scratch_shapes=scratch,
        compiler_params=pltpu.CompilerParams(
            dimension_semantics=("arbitrary",), vmem_limit_bytes=VMEM_LIMIT_BYTES),
        name="mixer_cache" if has_cache else "mixer",
    )(*args)


def _prep_layer(i, d, dk, dv, hd, w_ffn1_up, w_ffn1_down, w_in, w_gla_a2, w_out, w_ffn2_up, w_ffn2_down,
                w_ple_gate, w_ple):
    gk, kvw = GLA_HEADS * dk, SWA_KV_HEADS * hd
    widths = (d, d, gk, gk, d, d, GLA_RANK, d, kvw, kvw)
    offs = [0]
    for w in widths:
        offs.append(offs[-1] + w)
    part = lambda j: w_in[i][:, offs[j]:offs[j + 1]]
    order = (0, 1, 5, 4, 2, 3, 7, 8, 9)
    win = jnp.concatenate([part(j) for j in order], axis=1).astype(BF16)
    walr = jnp.pad(part(6), ((0, 0), (0, LANES - GLA_RANK))).astype(BF16)
    wa2 = jnp.pad(w_gla_a2[i], ((0, LANES - GLA_RANK), (0, 0))).astype(BF16)
    return dict(
        up1=w_ffn1_up[i].astype(BF16), dn1=w_ffn1_down[i].astype(BF16),
        up2=w_ffn2_up[i].astype(BF16), dn2=w_ffn2_down[i].astype(BF16),
        win=win, walr=walr, wa2=wa2, wout=w_out[i].astype(BF16),
        wpg=w_ple_gate[i].astype(BF16), wpl=w_ple[i].astype(BF16))


def kernel(x_prompt, x_sample, p_prompt, p_sample, cache_swa_k, cache_swa_v, state_gla, g_ffn1, w_ffn1_up, w_ffn1_down, g_mix, w_in, w_gla_a2, b_gla_a, g_gla, swa_sinks, w_out, g_ffn2, w_ffn2_up, w_ffn2_down, g_ple, w_ple_gate, w_ple, g_final):
    depth = w_in.shape[0]
    bp, sp, d = x_prompt.shape
    bs, ss, _ = x_sample.shape
    dk = state_gla.shape[-2]
    dv = state_gla.shape[-1]
    hd = cache_swa_k.shape[-1]
    kvw = SWA_KV_HEADS * hd
    wc = cache_swa_k.shape[2]
    assert wc == WINDOW and sp >= WINDOW

    xp = x_prompt.reshape(bp * sp, d)
    xs = x_sample.reshape(bs * ss, d)
    outs = [[] for _ in range(6)]
    for i in range(depth):
        w = _prep_layer(i, d, dk, dv, hd, w_ffn1_up, w_ffn1_down, w_in, w_gla_a2, w_out, w_ffn2_up,
                        w_ffn2_down, w_ple_gate, w_ple)
        last = g_final if i == depth - 1 else None
        mix = functools.partial(_mixer_call, gmix=g_mix[i], win=w["win"], walr=w["walr"], wa2=w["wa2"],
                                ba=b_gla_a[i], ggla=g_gla[i], sinks=swa_sinks[i], wout=w["wout"],
                                dk=dk, dv=dv, hd=hd)
        xp = _ffn_call(xp, g_ffn1[i], w["up1"], w["dn1"])
        xp3, pk, pv, ps = mix(xp.reshape(bp, sp, d), None)
        xp = _ffn_call(xp3.reshape(bp * sp, d), g_ffn2[i], w["up2"], w["dn2"],
                       ple=(p_prompt[i].reshape(bp * sp, -1), g_ple[i], w["wpg"], w["wpl"]), g_final=last)
        xs = _ffn_call(xs, g_ffn1[i], w["up1"], w["dn1"])
        cache = (cache_swa_k[i].reshape(bs, wc, kvw), cache_swa_v[i].reshape(bs, wc, kvw), state_gla[i])
        xs3, sk, sv, s_s = mix(xs.reshape(bs, ss, d), cache)
        xs = _ffn_call(xs3.reshape(bs * ss, d), g_ffn2[i], w["up2"], w["dn2"],
                       ple=(p_sample[i].reshape(bs * ss, -1), g_ple[i], w["wpg"], w["wpl"]), g_final=last)
        for lst, val in zip(outs, (pk.reshape(bp, WINDOW, SWA_KV_HEADS, hd), pv.reshape(bp, WINDOW, SWA_KV_HEADS, hd),
                                   ps, sk.reshape(bs, wc, SWA_KV_HEADS, hd), sv.reshape(bs, wc, SWA_KV_HEADS, hd), s_s)):
            lst.append(val)
    return (xp.reshape(bp, sp, d), xs.reshape(bs, ss, d)) + tuple(jnp.stack(o) for o in outs)
```

```python
import functools

import jax
import jax.numpy as jnp
from jax import lax
from jax.experimental import pallas as pl
from jax.experimental.pallas import tpu as pltpu

F32 = jnp.float32
BF16 = jnp.bfloat16

CHUNK = 64
EPS = 1e-6
GLA_HEADS = 4
GLA_RANK = 16
GLA_TAU = 16.0
SWA_HEADS = 16
SWA_KV_HEADS = 4
SWA_GROUP = SWA_HEADS // SWA_KV_HEADS
WINDOW = 128
BAND = WINDOW + CHUNK

LANES = 128
MXU_N = 256
VMEM_LIMIT_BYTES = 56 * 1024 * 1024

GLA_SAFE_LOG_DECAY = -60.0


def _sigmoid(x):
    return 1.0 / (1.0 + jnp.exp(-x))


def _log_sigmoid(x):
    return jnp.minimum(x, 0.0) - jnp.log(1.0 + jnp.exp(-jnp.abs(x)))


def _rms_rows(x, g):
    ms = jnp.mean(x * x, axis=-1, keepdims=True)
    return x * lax.rsqrt(ms + EPS) * g


def _dot(a, b):
    return jnp.dot(a, b, preferred_element_type=F32)


def _dot_nt(a, b):
    return lax.dot_general(a, b, (((1,), (1,)), ((), ())), preferred_element_type=F32)


def _norm_rows_to(src_ref, g_ref, dst_ref, rows, rb):
    g = g_ref[...]
    for r in range(0, rows, rb):
        dst_ref[r:r + rb, :] = _rms_rows(src_ref[r:r + rb, :], g).astype(dst_ref.dtype)


def _ffn_kernel(*refs, tm, d_ff, has_ple, has_final):
    it = iter(refs)
    x_ref, g_ref, wup_ref, wdn_ref = next(it), next(it), next(it), next(it)
    if has_ple:
        pe_ref, gple_ref, wpg_ref, wpl_ref = next(it), next(it), next(it), next(it)
    if has_final:
        gfin_ref = next(it)
    o_ref = next(it)
    xn_ref, act_ref = next(it), next(it)
    if has_ple:
        h_ref = next(it)
    d = x_ref.shape[1]
    rb = min(tm, 64)

    _norm_rows_to(x_ref, g_ref, xn_ref, tm, rb)

    for c0 in range(0, d_ff, MXU_N):
        xn = xn_ref[...]
        gate = _dot(xn, wup_ref[:, c0:c0 + MXU_N])
        up = _dot(xn, wup_ref[:, d_ff + c0:d_ff + c0 + MXU_N])
        act_ref[:, c0:c0 + MXU_N] = (gate * _sigmoid(gate) * up).astype(BF16)

    dst_ref = h_ref if has_ple else o_ref
    for n0 in range(0, d, MXU_N):
        y = _dot(act_ref[...], wdn_ref[:, n0:n0 + MXU_N])
        dst_ref[:, n0:n0 + MXU_N] = x_ref[:, n0:n0 + MXU_N] + 0.5 * y

    if has_ple:
        _norm_rows_to(h_ref, gple_ref, xn_ref, tm, rb)
        for n0 in range(0, d, MXU_N):
            gate = _sigmoid(_dot(xn_ref[...], wpg_ref[:, n0:n0 + MXU_N]))
            emb = _dot(pe_ref[...].astype(BF16), wpl_ref[:, n0:n0 + MXU_N])
            o_ref[:, n0:n0 + MXU_N] = h_ref[:, n0:n0 + MXU_N] + gate * emb

    if has_final:
        _norm_rows_to(o_ref, gfin_ref, o_ref, tm, rb)


def _const_spec(shape):
    zeros = (0,) * len(shape)
    return pl.BlockSpec(shape, lambda *_: zeros, pipeline_mode=pl.Buffered(1))


def _ffn_call(x, g, wup, wdn, ple=None, g_final=None):
    n, d = x.shape
    d_ff = wdn.shape[0]
    tm = min(n, 512)
    assert n % tm == 0 and d_ff % MXU_N == 0 and d % MXU_N == 0
    has_ple, has_final = ple is not None, g_final is not None
    row_spec = lambda w: pl.BlockSpec((tm, w), lambda i: (i, 0))
    args = [x, g.reshape(1, d), wup, wdn]
    specs = [row_spec(d), _const_spec((1, d)), _const_spec(wup.shape), _const_spec(wdn.shape)]
    scratch = [pltpu.VMEM((tm, d), BF16), pltpu.VMEM((tm, d_ff), BF16)]
    if has_ple:
        pe, gple, wpg, wpl = ple
        args += [pe, gple.reshape(1, d), wpg, wpl]
        specs += [row_spec(pe.shape[1]), _const_spec((1, d)), _const_spec(wpg.shape), _const_spec(wpl.shape)]
        scratch.append(pltpu.VMEM((tm, d), F32))
    if has_final:
        args.append(g_final.reshape(1, d))
        specs.append(_const_spec((1, d)))
    return pl.pallas_call(
        functools.partial(_ffn_kernel, tm=tm, d_ff=d_ff, has_ple=has_ple, has_final=has_final),
        out_shape=jax.ShapeDtypeStruct((n, d), F32),
        grid=(n // tm,),
        in_specs=specs,
        out_specs=row_spec(d),
        scratch_shapes=scratch,
        compiler_params=pltpu.CompilerParams(
            dimension_semantics=("arbitrary",), vmem_limit_bytes=VMEM_LIMIT_BYTES),
        name="ffn_ple" if has_ple else "ffn",
    )(*args)


def _mixer_kernel(*refs, t_tile, n_t, d, dk, dv, hd, has_cache):
    it = iter(refs)
    h_ref = next(it)
    if has_cache:
        ck_ref, cv_ref, s0_ref = next(it), next(it), next(it)
    gmix_ref, wina_ref, winb_ref, walr_ref, wa2_ref, ba_ref, ggla_ref, sink_ref, wout_ref = (
        next(it) for _ in range(9))
    o_ref, knew_ref, vnew_ref, sfin_ref = next(it), next(it), next(it), next(it)
    un_ref = next(it)
    z_refs, b_refs, h_refs = (next(it), next(it)), (next(it), next(it)), (next(it), next(it))
    mg_ref, kbuf_ref, vbuf_ref, s_ref, flag_ref = (next(it) for _ in range(5))

    g = pl.program_id(0)
    p = jnp.maximum(g - 1, 0)
    t_mix = lax.rem(p, n_t)
    gk = GLA_HEADS * dk
    kvw = SWA_KV_HEADS * hd
    gw = SWA_GROUP * hd
    n_chunks = t_tile // CHUNK
    c_ga, c_gb = 0, d
    c_qa = 2 * d
    c_ka = c_qa + gk
    c_va = c_ka + gk
    c_ra = c_va + d
    c_qb = c_ra + d
    c_kb = c_qb + d
    c_vb = c_kb + kvw
    z_w = c_vb + kvw
    zc = 2 * MXU_N
    rb = min(t_tile, 64)

    row = lax.broadcasted_iota(jnp.int32, (CHUNK, CHUNK), 0)
    col = lax.broadcasted_iota(jnp.int32, (CHUNK, CHUNK), 1)
    causal = row >= col
    tri = jnp.where(causal, 1.0, 0.0).astype(BF16)

    @pl.when(g == 0)
    def _first():
        z_refs[1][...] = jnp.zeros(z_refs[1].shape, F32)
        b_refs[1][...] = jnp.zeros(b_refs[1].shape, F32)
        h_refs[1][...] = jnp.zeros(h_refs[1].shape, F32)
        flag_ref[1] = 1

    @pl.when(t_mix == 0)
    def _init():
        if has_cache:
            s_ref[...] = s0_ref[0]
            kbuf_ref[0:WINDOW, :] = ck_ref[0]
            vbuf_ref[0:WINDOW, :] = cv_ref[0]
        else:
            s_ref[...] = jnp.zeros(s_ref.shape, F32)
            kbuf_ref[0:WINDOW, :] = jnp.zeros((WINDOW, kvw), F32)
            vbuf_ref[0:WINDOW, :] = jnp.zeros((WINDOW, kvw), F32)

    class Proj:
        def __init__(self, z_dst, b_dst, h_dst, flag_idx):
            self.z_dst, self.b_dst, self.h_dst, self.flag_idx = z_dst, b_dst, h_dst, flag_idx
            self.z_cols = list(range(0, z_w, zc))

        def norm(self):
            g_mix = gmix_ref[...]
            for r in range(0, t_tile, rb):
                x = h_ref[0, r:r + rb, :]
                self.h_dst[r:r + rb, :] = x
                un_ref[r:r + rb, :] = _rms_rows(x, g_mix).astype(BF16)

        def zproj(self, n=1):
            for _ in range(n):
                if self.z_cols:
                    c0 = self.z_cols.pop(0)
                    w = wina_ref[:, c0:c0 + zc] if c0 < c_qb else winb_ref[:, c0 - c_qb:c0 - c_qb + zc]
                    self.z_dst[:, c0:c0 + zc] = _dot(un_ref[...], w)

        def low_rank(self):
            self.alr = _dot(un_ref[...], walr_ref[...]).astype(BF16)

        def log_decay(self):
            self.la = _log_sigmoid(_dot(self.alr, wa2_ref[...]) + ba_ref[...]) * (1.0 / GLA_TAU)

        def cumulate(self):
            lows = []
            for r in range(0, t_tile, CHUNK):
                la_c = self.la[r:r + CHUNK, :]
                hi = la_c.astype(BF16)
                lo = (la_c - hi.astype(F32)).astype(BF16)
                b = _dot(tri, hi) + _dot(tri, lo)
                self.b_dst[r:r + CHUNK, :] = b
                lows.append(b[CHUNK - 1:CHUNK, :])
            low = functools.reduce(jnp.minimum, lows)
            flag_ref[self.flag_idx] = (jnp.min(low) >= GLA_SAFE_LOG_DECAY).astype(jnp.int32)

        def rest(self):
            self.zproj(len(self.z_cols))

    class Chunk:
        def __init__(self, z_src, b_src, r0, c, exact, masked):
            self.z, self.b_src, self.r0, self.c, self.exact, self.masked = z_src, b_src, r0, c, exact, masked
            self.rows = pl.ds(r0, CHUNK)

        def pre(self):
            z, rows = self.z, self.rows
            b = self.b_src[rows, :]
            b_last = b[CHUNK - 1:CHUNK, :]
            q = z[rows, c_qa:c_qa + gk] * (dk ** -0.5)
            k = z[rows, c_ka:c_ka + gk]
            qg = q * jnp.exp(b)
            kdec = k * jnp.exp(b_last - b)
            a_last = jnp.exp(b_last)
            self.b, self.q, self.qgb = b, q, qg.astype(BF16)
            if not self.exact:
                self.kt = (k * jnp.exp(-b)).astype(BF16)
            pad = jnp.zeros((LANES - CHUNK - 8, dk), F32)
            self.kdT, self.acol, self.v = [], [], []
            for hh in range(GLA_HEADS):
                stacked = jnp.concatenate(
                    [kdec[:, hh * dk:(hh + 1) * dk],
                     jnp.broadcast_to(a_last[:, hh * dk:(hh + 1) * dk], (8, dk)), pad], axis=0)
                st = stacked.T
                self.kdT.append(st[:, 0:CHUNK].astype(BF16))
                self.acol.append(st[:, CHUNK:CHUNK + 1])
                self.v.append(z[rows, c_va + hh * dv:c_va + (hh + 1) * dv].astype(BF16))
            band = pl.ds(self.r0, BAND)
            self.kb, self.vb, self.qs = [], [], []
            for grp in range(SWA_KV_HEADS):
                self.kb.append(kbuf_ref[band, grp * hd:(grp + 1) * hd].astype(BF16))
                self.vb.append(vbuf_ref[band, grp * hd:(grp + 1) * hd].astype(BF16))
                qgrp = z[rows, c_qb + grp * gw:c_qb + (grp + 1) * gw] * (hd ** -0.5)
                qs = jnp.concatenate([qgrp[:, m * hd:(m + 1) * hd] for m in range(SWA_GROUP)], axis=0)
                self.qs.append(qs.astype(BF16))

        def mm1(self):
            if self.exact:
                q, b, r0, z = self.q, self.b, self.r0, self.z

                def key_body(j, accs):
                    bj = self.b_src[pl.ds(r0 + j, 1), :]
                    kj = z[pl.ds(r0 + j, 1), c_ka:c_ka + gk]
                    w = q * jnp.exp(jnp.minimum(b - bj, 0.0)) * kj
                    out = []
                    for hh in range(GLA_HEADS):
                        colsum = jnp.sum(w[:, hh * dk:(hh + 1) * dk], axis=1, keepdims=True)
                        out.append(jnp.where((col == j) & causal, colsum, accs[hh]))
                    return tuple(out)

                self.scores = lax.fori_loop(0, CHUNK, key_body,
                                            tuple(jnp.zeros((CHUNK, CHUNK), F32) for _ in range(GLA_HEADS)))
            else:
                self.scores = [_dot_nt(self.qgb[:, hh * dk:(hh + 1) * dk], self.kt[:, hh * dk:(hh + 1) * dk])
                               for hh in range(GLA_HEADS)]
            self.s_qk = [_dot_nt(self.qs[grp], self.kb[grp]) for grp in range(SWA_KV_HEADS)]
            self.o_inter = []
            for hh in range(GLA_HEADS):
                s_old = s_ref[hh]
                self.o_inter.append(_dot(self.qgb[:, hh * dk:(hh + 1) * dk], s_old.astype(BF16)))
                s_ref[hh] = s_old * self.acol[hh] + _dot(self.kdT[hh], self.v[hh])

        def post1(self):
            if not self.exact:
                self.scores = [jnp.where(causal, s, 0.0) for s in self.scores]
            self.scores = [s.astype(BF16) for s in self.scores]
            if self.masked:
                cg = t_mix * n_chunks + self.c
                kcol = lax.broadcasted_iota(jnp.int32, (CHUNK, BAND), 1)
                valid = kcol >= WINDOW - CHUNK * jnp.minimum(cg, WINDOW // CHUNK)
            self.p, self.den = [], []
            for grp in range(SWA_KV_HEADS):
                ps, dens = [], []
                for m in range(SWA_GROUP):
                    sm = self.s_qk[grp][m * CHUNK:(m + 1) * CHUNK, :]
                    if self.masked:
                        sm = jnp.where(valid, sm, -jnp.inf)
                    sink = sink_ref[grp * SWA_GROUP + m]
                    mx = jnp.maximum(jnp.max(sm, axis=1, keepdims=True), sink)
                    pm = jnp.exp(sm - mx)
                    dens.append(jnp.sum(pm, axis=1, keepdims=True) + jnp.exp(sink - mx))
                    ps.append(pm.astype(BF16))
                self.p.append(jnp.concatenate(ps, axis=0))
                self.den.append(dens)

        def mm2(self):
            self.o = [_dot(self.scores[hh], self.v[hh]) + self.o_inter[hh] for hh in range(GLA_HEADS)]
            self.og = [_dot(self.p[grp], self.vb[grp]) for grp in range(SWA_KV_HEADS)]

        def post2(self):
            z, rows = self.z, self.rows
            g_gla = ggla_ref[...]
            for j in range(GLA_HEADS):
                o = _rms_rows(self.o[j], g_gla[:, j * dv:(j + 1) * dv])
                ra = z[rows, c_ra + j * dv:c_ra + (j + 1) * dv]
                ga = z[rows, c_ga + j * dv:c_ga + (j + 1) * dv]
                gb = z[rows, c_gb + j * gw:c_gb + (j + 1) * gw]
                ob = jnp.concatenate([self.og[j][m * CHUNK:(m + 1) * CHUNK, :] / self.den[j][m]
                                      for m in range(SWA_GROUP)], axis=1)
                mg_ref[rows, j * dv:(j + 1) * dv] = (
                    _sigmoid(ga) * (o * (ra * _sigmoid(ra))) + _sigmoid(gb) * ob).astype(BF16)

    def stage_kv(z_src):
        kbuf_ref[WINDOW:WINDOW + t_tile, :] = z_src[:, c_kb:c_kb + kvw]
        vbuf_ref[WINDOW:WINDOW + t_tile, :] = z_src[:, c_vb:c_vb + kvw]

    def out_proj(h_src):
        for n0 in range(0, d, MXU_N):
            o_ref[0, :, n0:n0 + MXU_N] = h_src[:, n0:n0 + MXU_N] + _dot(mg_ref[...], wout_ref[:, n0:n0 + MXU_N])

    def step(slot):
        z_dst, b_dst, h_dst = z_refs[slot], b_refs[slot], h_refs[slot]
        z_src, b_src, h_src = z_refs[1 - slot], b_refs[1 - slot], h_refs[1 - slot]
        safe = flag_ref[1 - slot] == 1

        @pl.when(safe)
        def _fused():
            proj = Proj(z_dst, b_dst, h_dst, slot)
            proj.norm()
            stage_kv(z_src)
            for c in range(n_chunks):
                ch = Chunk(z_src, b_src, c * CHUNK, c, exact=False,
                           masked=(not has_cache) and c < WINDOW // CHUNK)
                ch.pre()
                proj.zproj()
                ch.mm1()
                if c == 0:
                    proj.low_rank()
                proj.zproj(2)
                if c == min(1, n_chunks - 1):
                    proj.log_decay()
                ch.post1()
                ch.mm2()
                ch.post2()
            proj.rest()
            proj.cumulate()
            out_proj(h_src)

        @pl.when(jnp.logical_not(safe))
        def _exact():
            proj = Proj(z_dst, b_dst, h_dst, slot)
            proj.norm()
            proj.rest()
            proj.low_rank()
            proj.log_decay()
            proj.cumulate()
            stage_kv(z_src)

            def chunk_body(c, carry):
                ch = Chunk(z_src, b_src, pl.multiple_of(c * CHUNK, CHUNK), c, exact=True, masked=not has_cache)
                ch.pre()
                ch.mm1()
                ch.post1()
                ch.mm2()
                ch.post2()
                return carry

            lax.fori_loop(0, n_chunks, chunk_body, 0)
            out_proj(h_src)

    parity = lax.rem(g, 2)
    pl.when(parity == 0)(functools.partial(step, 0))
    pl.when(parity == 1)(functools.partial(step, 1))

    k_tail = kbuf_ref[t_tile:t_tile + WINDOW, :]
    v_tail = vbuf_ref[t_tile:t_tile + WINDOW, :]
    kbuf_ref[0:WINDOW, :] = k_tail
    vbuf_ref[0:WINDOW, :] = v_tail

    @pl.when(t_mix == n_t - 1)
    def _emit():
        knew_ref[0] = k_tail
        vnew_ref[0] = v_tail
        sfin_ref[0] = s_ref[...]


def _mixer_call(h, cache, gmix, wina, winb, walr, wa2, ba, ggla, sinks, wout, *, dk, dv, hd):
    bsz, seq, d = h.shape
    t_tile = min(seq, 256)
    assert seq % t_tile == 0 and t_tile % CHUNK == 0
    n_t = seq // t_tile
    n_tiles = bsz * n_t
    has_cache = cache is not None
    kvw = SWA_KV_HEADS * hd
    gk = GLA_HEADS * dk
    z_w = wina.shape[1] + winb.shape[1]
    assert wina.shape[1] % (2 * MXU_N) == 0 and winb.shape[1] % (2 * MXU_N) == 0
    assert GLA_HEADS * dv == d and SWA_HEADS * hd == d and SWA_GROUP * hd == dv

    def proj_tile(g):
        q = jnp.minimum(g, n_tiles - 1)
        return (q // n_t, q % n_t, 0)

    def mix_tile(g):
        q = jnp.maximum(g - 1, 0)
        return (q // n_t, q % n_t, 0)

    per_seq = lambda *shape: pl.BlockSpec((1,) + shape, lambda g: (jnp.maximum(g - 1, 0) // n_t,) + (0,) * len(shape))
    args, specs = [h], [pl.BlockSpec((1, t_tile, d), proj_tile)]
    if has_cache:
        ck, cv, s0 = cache
        args += [ck, cv, s0]
        specs += [per_seq(WINDOW, kvw), per_seq(WINDOW, kvw), per_seq(GLA_HEADS, dk, dv)]
    args += [gmix.reshape(1, d), wina, winb, walr, wa2, ba.reshape(1, gk), ggla.reshape(1, d), sinks, wout]
    specs += [_const_spec((1, d)), _const_spec(wina.shape), _const_spec(winb.shape), _const_spec(walr.shape),
              _const_spec(wa2.shape),
              _const_spec((1, gk)), _const_spec((1, d)), pl.BlockSpec(memory_space=pltpu.SMEM),
              _const_spec(wout.shape)]
    out_shape = (jax.ShapeDtypeStruct((bsz, seq, d), F32),
                 jax.ShapeDtypeStruct((bsz, WINDOW, kvw), F32),
                 jax.ShapeDtypeStruct((bsz, WINDOW, kvw), F32),
                 jax.ShapeDtypeStruct((bsz, GLA_HEADS, dk, dv), F32))
    out_specs = (pl.BlockSpec((1, t_tile, d), mix_tile), per_seq(WINDOW, kvw), per_seq(WINDOW, kvw),
                 per_seq(GLA_HEADS, dk, dv))
    scratch = [pltpu.VMEM((t_tile, d), BF16)]
    scratch += [pltpu.VMEM((t_tile, z_w), F32)] * 2
    scratch += [pltpu.VMEM((t_tile, gk), F32)] * 2
    scratch += [pltpu.VMEM((t_tile, d), F32)] * 2
    scratch += [
        pltpu.VMEM((t_tile, d), BF16),
        pltpu.VMEM((WINDOW + t_tile, kvw), F32),
        pltpu.VMEM((WINDOW + t_tile, kvw), F32),
        pltpu.VMEM((GLA_HEADS, dk, dv), F32),
        pltpu.SMEM((2,), jnp.int32),
    ]
    return pl.pallas_call(
        functools.partial(_mixer_kernel, t_tile=t_tile, n_t=n_t, d=d, dk=dk, dv=dv, hd=hd, has_cache=has_cache),
        out_shape=out_shape,
        grid=(n_tiles + 1,),
        in_specs=specs,
        out_specs=out_specs,
        scratch_shapes=scratch,
        compiler_params=pltpu.CompilerParams(
            dimension_semantics=("arbitrary",), vmem_limit_bytes=VMEM_LIMIT_BYTES),
        name="mixer_cache" if has_cache else "mixer",
    )(*args)


def _prep_layer(i, d, dk, dv, hd, w_ffn1_up, w_ffn1_down, w_in, w_gla_a2, w_out, w_ffn2_up, w_ffn2_down,
                w_ple_gate, w_ple):
    gk = GLA_HEADS * dk
    n_a = 4 * d + 2 * gk
    wina = w_in[i][:, :n_a].astype(BF16)
    winb = w_in[i][:, n_a + GLA_RANK:].astype(BF16)
    walr = jnp.pad(w_in[i][:, n_a:n_a + GLA_RANK], ((0, 0), (0, LANES - GLA_RANK))).astype(BF16)
    wa2 = jnp.pad(w_gla_a2[i], ((0, LANES - GLA_RANK), (0, 0))).astype(BF16)
    return dict(
        up1=w_ffn1_up[i].astype(BF16), dn1=w_ffn1_down[i].astype(BF16),
        up2=w_ffn2_up[i].astype(BF16), dn2=w_ffn2_down[i].astype(BF16),
        wina=wina, winb=winb, walr=walr, wa2=wa2, wout=w_out[i].astype(BF16),
        wpg=w_ple_gate[i].astype(BF16), wpl=w_ple[i].astype(BF16))


def kernel(x_prompt, x_sample, p_prompt, p_sample, cache_swa_k, cache_swa_v, state_gla, g_ffn1, w_ffn1_up, w_ffn1_down, g_mix, w_in, w_gla_a2, b_gla_a, g_gla, swa_sinks, w_out, g_ffn2, w_ffn2_up, w_ffn2_down, g_ple, w_ple_gate, w_ple, g_final):
    depth = w_in.shape[0]
    bp, sp, d = x_prompt.shape
    bs, ss, _ = x_sample.shape
    dk = state_gla.shape[-2]
    dv = state_gla.shape[-1]
    hd = cache_swa_k.shape[-1]
    kvw = SWA_KV_HEADS * hd
    wc = cache_swa_k.shape[2]
    assert wc == WINDOW and sp >= WINDOW

    xp = x_prompt.reshape(bp * sp, d)
    xs = x_sample.reshape(bs * ss, d)
    outs = [[] for _ in range(6)]
    for i in range(depth):
        w = _prep_layer(i, d, dk, dv, hd, w_ffn1_up, w_ffn1_down, w_in, w_gla_a2, w_out, w_ffn2_up,
                        w_ffn2_down, w_ple_gate, w_ple)
        last = g_final if i == depth - 1 else None
        mix = functools.partial(_mixer_call, gmix=g_mix[i], wina=w["wina"], winb=w["winb"], walr=w["walr"], wa2=w["wa2"],
                                ba=b_gla_a[i], ggla=g_gla[i], sinks=swa_sinks[i], wout=w["wout"],
                                dk=dk, dv=dv, hd=hd)
        xp = _ffn_call(xp, g_ffn1[i], w["up1"], w["dn1"])
        xp3, pk, pv, ps = mix(xp.reshape(bp, sp, d), None)
        xp = _ffn_call(xp3.reshape(bp * sp, d), g_ffn2[i], w["up2"], w["dn2"],
                       ple=(p_prompt[i].reshape(bp * sp, -1), g_ple[i], w["wpg"], w["wpl"]), g_final=last)
        xs = _ffn_call(xs, g_ffn1[i], w["up1"], w["dn1"])
        cache = (cache_swa_k[i].reshape(bs, wc, kvw), cache_swa_v[i].reshape(bs, wc, kvw), state_gla[i])
        xs3, sk, sv, s_s = mix(xs.reshape(bs, ss, d), cache)
        xs = _ffn_call(xs3.reshape(bs * ss, d), g_ffn2[i], w["up2"], w["dn2"],
                       ple=(p_sample[i].reshape(bs * ss, -1), g_ple[i], w["wpg"], w["wpl"]), g_final=last)
        for lst, val in zip(outs, (pk.reshape(bp, WINDOW, SWA_KV_HEADS, hd), pv.reshape(bp, WINDOW, SWA_KV_HEADS, hd),
                                   ps, sk.reshape(bs, wc, SWA_KV_HEADS, hd), sv.reshape(bs, wc, SWA_KV_HEADS, hd), s_s)):
            lst.append(val)
    return (xp.reshape(bp, sp, d), xs.reshape(bs, ss, d)) + tuple(jnp.stack(o) for o in outs)
```

```python
import functools

import jax
import jax.numpy as jnp
from jax import lax
from jax.experimental import pallas as pl
from jax.experimental.pallas import tpu as pltpu

F32 = jnp.float32
BF16 = jnp.bfloat16

CHUNK = 64
EPS = 1e-6
GLA_HEADS = 4
GLA_RANK = 16
GLA_TAU = 16.0
SWA_HEADS = 16
SWA_KV_HEADS = 4
SWA_GROUP = SWA_HEADS // SWA_KV_HEADS
WINDOW = 128
BAND = WINDOW + CHUNK

LANES = 128
MXU_N = 256
VMEM_LIMIT_BYTES = 56 * 1024 * 1024

GLA_SAFE_LOG_DECAY = -60.0


def _sigmoid(x):
    return 1.0 / (1.0 + jnp.exp(-x))


def _log_sigmoid(x):
    return jnp.minimum(x, 0.0) - jnp.log(1.0 + jnp.exp(-jnp.abs(x)))


def _rms_rows(x, g):
    ms = jnp.mean(x * x, axis=-1, keepdims=True)
    return x * lax.rsqrt(ms + EPS) * g


def _dot(a, b):
    return jnp.dot(a, b, preferred_element_type=F32)


def _dot_nt(a, b):
    return lax.dot_general(a, b, (((1,), (1,)), ((), ())), preferred_element_type=F32)


def _norm_rows_to(src_ref, g_ref, dst_ref, rows, rb):
    g = g_ref[...]
    for r in range(0, rows, rb):
        dst_ref[r:r + rb, :] = _rms_rows(src_ref[r:r + rb, :], g).astype(dst_ref.dtype)


def _ffn_kernel(*refs, tm, d_ff, has_ple, has_final):
    it = iter(refs)
    x_ref, g_ref, wup_ref, wdn_ref = next(it), next(it), next(it), next(it)
    if has_ple:
        pe_ref, gple_ref, wpg_ref, wpl_ref = next(it), next(it), next(it), next(it)
    if has_final:
        gfin_ref = next(it)
    o_ref = next(it)
    xn_ref, act_ref = next(it), next(it)
    if has_ple:
        h_ref = next(it)
    d = x_ref.shape[1]
    rb = min(tm, 64)

    _norm_rows_to(x_ref, g_ref, xn_ref, tm, rb)

    for c0 in range(0, d_ff, MXU_N):
        xn = xn_ref[...]
        gate = _dot(xn, wup_ref[:, c0:c0 + MXU_N])
        up = _dot(xn, wup_ref[:, d_ff + c0:d_ff + c0 + MXU_N])
        act_ref[:, c0:c0 + MXU_N] = (gate * _sigmoid(gate) * up).astype(BF16)

    dst_ref = h_ref if has_ple else o_ref
    for n0 in range(0, d, MXU_N):
        y = _dot(act_ref[...], wdn_ref[:, n0:n0 + MXU_N])
        dst_ref[:, n0:n0 + MXU_N] = x_ref[:, n0:n0 + MXU_N] + 0.5 * y

    if has_ple:
        _norm_rows_to(h_ref, gple_ref, xn_ref, tm, rb)
        for n0 in range(0, d, MXU_N):
            gate = _sigmoid(_dot(xn_ref[...], wpg_ref[:, n0:n0 + MXU_N]))
            emb = _dot(pe_ref[...].astype(BF16), wpl_ref[:, n0:n0 + MXU_N])
            o_ref[:, n0:n0 + MXU_N] = h_ref[:, n0:n0 + MXU_N] + gate * emb

    if has_final:
        _norm_rows_to(o_ref, gfin_ref, o_ref, tm, rb)


def _const_spec(shape):
    zeros = (0,) * len(shape)
    return pl.BlockSpec(shape, lambda *_: zeros, pipeline_mode=pl.Buffered(1))


def _layer_spec(stacked, layer, cols=None):
    _, rows, width = stacked.shape
    return pl.BlockSpec((None, rows, width if cols is None else cols), lambda *_: (layer, 0, 0),
                        pipeline_mode=pl.Buffered(1))


def _ffn_call(x, g, wup, wdn, layer, ple=None, g_final=None):
    n, d = x.shape
    d_ff = wdn.shape[1]
    tm = min(n, 512)
    assert n % tm == 0 and d_ff % MXU_N == 0 and d % MXU_N == 0
    has_ple, has_final = ple is not None, g_final is not None
    row_spec = lambda w: pl.BlockSpec((tm, w), lambda i: (i, 0))
    args = [x, g.reshape(1, d), wup, wdn]
    specs = [row_spec(d), _const_spec((1, d)), _layer_spec(wup, layer), _layer_spec(wdn, layer)]
    scratch = [pltpu.VMEM((tm, d), BF16), pltpu.VMEM((tm, d_ff), BF16)]
    if has_ple:
        pe, gple, wpg, wpl = ple
        args += [pe, gple.reshape(1, d), wpg, wpl]
        specs += [pl.BlockSpec((None, tm, pe.shape[2]), lambda i: (layer, i, 0)), _const_spec((1, d)),
                  _layer_spec(wpg, layer), _layer_spec(wpl, layer)]
        scratch.append(pltpu.VMEM((tm, d), F32))
    if has_final:
        args.append(g_final.reshape(1, d))
        specs.append(_const_spec((1, d)))
    return pl.pallas_call(
        functools.partial(_ffn_kernel, tm=tm, d_ff=d_ff, has_ple=has_ple, has_final=has_final),
        out_shape=jax.ShapeDtypeStruct((n, d), F32),
        grid=(n // tm,),
        in_specs=specs,
        out_specs=row_spec(d),
        scratch_shapes=scratch,
        compiler_params=pltpu.CompilerParams(
            dimension_semantics=("arbitrary",), vmem_limit_bytes=VMEM_LIMIT_BYTES),
        name="ffn_ple" if has_ple else "ffn",
    )(*args)


def _mixer_kernel(*refs, t_tile, n_t, d, dk, dv, hd, has_cache):
    it = iter(refs)
    h_ref = next(it)
    if has_cache:
        ck_ref, cv_ref, s0_ref = next(it), next(it), next(it)
    gmix_ref, wina_ref, winb_ref, walr_ref, wa2_ref, ba_ref, ggla_ref, sink_ref, wout_ref = (
        next(it) for _ in range(9))
    o_ref, knew_ref, vnew_ref, sfin_ref = next(it), next(it), next(it), next(it)
    un_ref = next(it)
    z_refs, b_refs, h_refs = (next(it), next(it)), (next(it), next(it)), (next(it), next(it))
    mg_ref, kbuf_ref, vbuf_ref, s_ref, flag_ref = (next(it) for _ in range(5))

    g = pl.program_id(0)
    p = jnp.maximum(g - 1, 0)
    t_mix = lax.rem(p, n_t)
    gk = GLA_HEADS * dk
    kvw = SWA_KV_HEADS * hd
    gw = SWA_GROUP * hd
    n_chunks = t_tile // CHUNK
    c_ga, c_gb = 0, d
    c_qa = 2 * d
    c_ka = c_qa + gk
    c_va = c_ka + gk
    c_ra = c_va + d
    c_qb = c_ra + d
    c_kb = c_qb + d
    c_vb = c_kb + kvw
    z_w = c_vb + kvw
    zc = 2 * MXU_N
    rb = min(t_tile, 64)

    row = lax.broadcasted_iota(jnp.int32, (CHUNK, CHUNK), 0)
    col = lax.broadcasted_iota(jnp.int32, (CHUNK, CHUNK), 1)
    causal = row >= col
    tri = jnp.where(causal, 1.0, 0.0).astype(BF16)

    @pl.when(g == 0)
    def _first():
        z_refs[1][...] = jnp.zeros(z_refs[1].shape, F32)
        b_refs[1][...] = jnp.zeros(b_refs[1].shape, F32)
        h_refs[1][...] = jnp.zeros(h_refs[1].shape, F32)
        flag_ref[1] = 1

    @pl.when(t_mix == 0)
    def _init():
        if has_cache:
            s_ref[...] = s0_ref[0]
            kbuf_ref[0:WINDOW, :] = ck_ref[0]
            vbuf_ref[0:WINDOW, :] = cv_ref[0]
        else:
            s_ref[...] = jnp.zeros(s_ref.shape, F32)
            kbuf_ref[0:WINDOW, :] = jnp.zeros((WINDOW, kvw), F32)
            vbuf_ref[0:WINDOW, :] = jnp.zeros((WINDOW, kvw), F32)

    class Proj:
        def __init__(self, z_dst, b_dst, h_dst, flag_idx):
            self.z_dst, self.b_dst, self.h_dst, self.flag_idx = z_dst, b_dst, h_dst, flag_idx
            self.z_cols = list(range(0, z_w, zc))

        def norm(self):
            g_mix = gmix_ref[...]
            for r in range(0, t_tile, rb):
                x = h_ref[0, r:r + rb, :]
                self.h_dst[r:r + rb, :] = x
                un_ref[r:r + rb, :] = _rms_rows(x, g_mix).astype(BF16)

        def zproj(self, n=1):
            for _ in range(n):
                if self.z_cols:
                    c0 = self.z_cols.pop(0)
                    w = wina_ref[:, c0:c0 + zc] if c0 < c_qb else winb_ref[:, c0 - c_qb:c0 - c_qb + zc]
                    self.z_dst[:, c0:c0 + zc] = _dot(un_ref[...], w)

        def low_rank(self):
            self.alr = _dot(un_ref[...], walr_ref[...]).astype(BF16)

        def log_decay(self):
            self.la = _log_sigmoid(_dot(self.alr, wa2_ref[...]) + ba_ref[...]) * (1.0 / GLA_TAU)

        def cumulate(self):
            lows = []
            for r in range(0, t_tile, CHUNK):
                la_c = self.la[r:r + CHUNK, :]
                hi = la_c.astype(BF16)
                lo = (la_c - hi.astype(F32)).astype(BF16)
                b = _dot(tri, hi) + _dot(tri, lo)
                self.b_dst[r:r + CHUNK, :] = b
                lows.append(b[CHUNK - 1:CHUNK, :])
            low = functools.reduce(jnp.minimum, lows)
            flag_ref[self.flag_idx] = (jnp.min(low) >= GLA_SAFE_LOG_DECAY).astype(jnp.int32)

        def rest(self):
            self.zproj(len(self.z_cols))

        def all_rolled(self):
            g_mix = gmix_ref[...]

            def norm_body(i, carry):
                r = pl.multiple_of(i * rb, rb)
                x = h_ref[0, pl.ds(r, rb), :]
                self.h_dst[pl.ds(r, rb), :] = x
                un_ref[pl.ds(r, rb), :] = _rms_rows(x, g_mix).astype(BF16)
                return carry

            lax.fori_loop(0, t_tile // rb, norm_body, 0)
            for w_ref, base in ((wina_ref, 0), (winb_ref, c_qb)):
                def z_body(i, carry, w_ref=w_ref, base=base):
                    c0 = pl.multiple_of(i * zc, zc)
                    self.z_dst[:, pl.ds(base + c0, zc)] = _dot(un_ref[...], w_ref[:, pl.ds(c0, zc)])
                    return carry

                lax.fori_loop(0, w_ref.shape[1] // zc, z_body, 0)
            self.low_rank()
            self.log_decay()
            self.b_dst[...] = self.la

            def cum_body(c, low):
                rows = pl.ds(pl.multiple_of(c * CHUNK, CHUNK), CHUNK)
                la_c = self.b_dst[rows, :]
                hi = la_c.astype(BF16)
                b = _dot(tri, hi) + _dot(tri, (la_c - hi.astype(F32)).astype(BF16))
                self.b_dst[rows, :] = b
                return jnp.minimum(low, b[CHUNK - 1:CHUNK, :])

            low = lax.fori_loop(0, n_chunks, cum_body, jnp.zeros((1, gk), F32))
            flag_ref[self.flag_idx] = (jnp.min(low) >= GLA_SAFE_LOG_DECAY).astype(jnp.int32)

    class Chunk:
        def __init__(self, z_src, b_src, r0, c, exact, masked):
            self.z, self.b_src, self.r0, self.c, self.exact, self.masked = z_src, b_src, r0, c, exact, masked
            self.rows = pl.ds(r0, CHUNK)

        def pre(self):
            z, rows = self.z, self.rows
            b = self.b_src[rows, :]
            b_last = b[CHUNK - 1:CHUNK, :]
            q = z[rows, c_qa:c_qa + gk] * (dk ** -0.5)
            k = z[rows, c_ka:c_ka + gk]
            qg = q * jnp.exp(b)
            kdec = k * jnp.exp(b_last - b)
            a_last = jnp.exp(b_last)
            self.b, self.q, self.qgb = b, q, qg.astype(BF16)
            if not self.exact:
                self.kt = (k * jnp.exp(-b)).astype(BF16)
            pad = jnp.zeros((LANES - CHUNK - 8, dk), F32)
            self.kdT, self.acol, self.v = [], [], []
            for hh in range(GLA_HEADS):
                stacked = jnp.concatenate(
                    [kdec[:, hh * dk:(hh + 1) * dk],
                     jnp.broadcast_to(a_last[:, hh * dk:(hh + 1) * dk], (8, dk)), pad], axis=0)
                st = stacked.T
                self.kdT.append(st[:, 0:CHUNK].astype(BF16))
                self.acol.append(st[:, CHUNK:CHUNK + 1])
                self.v.append(z[rows, c_va + hh * dv:c_va + (hh + 1) * dv].astype(BF16))
            band = pl.ds(self.r0, BAND)
            self.kb, self.vb, self.qs = [], [], []
            for grp in range(SWA_KV_HEADS):
                self.kb.append(kbuf_ref[band, grp * hd:(grp + 1) * hd].astype(BF16))
                self.vb.append(vbuf_ref[band, grp * hd:(grp + 1) * hd].astype(BF16))
                qgrp = z[rows, c_qb + grp * gw:c_qb + (grp + 1) * gw] * (hd ** -0.5)
                qs = jnp.concatenate([qgrp[:, m * hd:(m + 1) * hd] for m in range(SWA_GROUP)], axis=0)
                self.qs.append(qs.astype(BF16))

        def mm1(self):
            if self.exact:
                q, b, r0, z = self.q, self.b, self.r0, self.z

                def key_body(j, accs):
                    bj = self.b_src[pl.ds(r0 + j, 1), :]
                    kj = z[pl.ds(r0 + j, 1), c_ka:c_ka + gk]
                    w = q * jnp.exp(jnp.minimum(b - bj, 0.0)) * kj
                    out = []
                    for hh in range(GLA_HEADS):
                        colsum = jnp.sum(w[:, hh * dk:(hh + 1) * dk], axis=1, keepdims=True)
                        out.append(jnp.where((col == j) & causal, colsum, accs[hh]))
                    return tuple(out)

                self.scores = lax.fori_loop(0, CHUNK, key_body,
                                            tuple(jnp.zeros((CHUNK, CHUNK), F32) for _ in range(GLA_HEADS)))
            else:
                self.scores = [_dot_nt(self.qgb[:, hh * dk:(hh + 1) * dk], self.kt[:, hh * dk:(hh + 1) * dk])
                               for hh in range(GLA_HEADS)]
            self.s_qk = [_dot_nt(self.qs[grp], self.kb[grp]) for grp in range(SWA_KV_HEADS)]
            self.o_inter = []
            for hh in range(GLA_HEADS):
                s_old = s_ref[hh]
                self.o_inter.append(_dot(self.qgb[:, hh * dk:(hh + 1) * dk], s_old.astype(BF16)))
                s_ref[hh] = s_old * self.acol[hh] + _dot(self.kdT[hh], self.v[hh])

        def post1(self):
            if not self.exact:
                self.scores = [jnp.where(causal, s, 0.0) for s in self.scores]
            self.scores = [s.astype(BF16) for s in self.scores]
            if self.masked:
                cg = t_mix * n_chunks + self.c
                kcol = lax.broadcasted_iota(jnp.int32, (CHUNK, BAND), 1)
                valid = kcol >= WINDOW - CHUNK * jnp.minimum(cg, WINDOW // CHUNK)
            self.p, self.den = [], []
            for grp in range(SWA_KV_HEADS):
                ps, dens = [], []
                for m in range(SWA_GROUP):
                    sm = self.s_qk[grp][m * CHUNK:(m + 1) * CHUNK, :]
                    if self.masked:
                        sm = jnp.where(valid, sm, -jnp.inf)
                    sink = sink_ref[grp * SWA_GROUP + m]
                    mx = jnp.maximum(jnp.max(sm, axis=1, keepdims=True), sink)
                    pm = jnp.exp(sm - mx)
                    dens.append(jnp.sum(pm, axis=1, keepdims=True) + jnp.exp(sink - mx))
                    ps.append(pm.astype(BF16))
                self.p.append(jnp.concatenate(ps, axis=0))
                self.den.append(dens)

        def mm2(self):
            self.o = [_dot(self.scores[hh], self.v[hh]) + self.o_inter[hh] for hh in range(GLA_HEADS)]
            self.og = [_dot(self.p[grp], self.vb[grp]) for grp in range(SWA_KV_HEADS)]

        def post2(self):
            z, rows = self.z, self.rows
            g_gla = ggla_ref[...]
            for j in range(GLA_HEADS):
                o = _rms_rows(self.o[j], g_gla[:, j * dv:(j + 1) * dv])
                ra = z[rows, c_ra + j * dv:c_ra + (j + 1) * dv]
                ga = z[rows, c_ga + j * dv:c_ga + (j + 1) * dv]
                gb = z[rows, c_gb + j * gw:c_gb + (j + 1) * gw]
                ob = jnp.concatenate([self.og[j][m * CHUNK:(m + 1) * CHUNK, :] / self.den[j][m]
                                      for m in range(SWA_GROUP)], axis=1)
                mg_ref[rows, j * dv:(j + 1) * dv] = (
                    _sigmoid(ga) * (o * (ra * _sigmoid(ra))) + _sigmoid(gb) * ob).astype(BF16)

    def stage_kv(z_src):
        kbuf_ref[WINDOW:WINDOW + t_tile, :] = z_src[:, c_kb:c_kb + kvw]
        vbuf_ref[WINDOW:WINDOW + t_tile, :] = z_src[:, c_vb:c_vb + kvw]

    def out_proj(h_src, rolled=False):
        def body(n, carry):
            cols = pl.ds(n * MXU_N, MXU_N) if isinstance(n, int) else pl.ds(pl.multiple_of(n * MXU_N, MXU_N), MXU_N)
            o_ref[0, :, cols] = h_src[:, cols] + _dot(mg_ref[...], wout_ref[:, cols])
            return carry

        if rolled:
            lax.fori_loop(0, d // MXU_N, body, 0)
        else:
            for n in range(d // MXU_N):
                body(n, 0)

    def step(slot):
        z_dst, b_dst, h_dst = z_refs[slot], b_refs[slot], h_refs[slot]
        z_src, b_src, h_src = z_refs[1 - slot], b_refs[1 - slot], h_refs[1 - slot]
        safe = flag_ref[1 - slot] == 1

        @pl.when(safe)
        def _fused():
            proj = Proj(z_dst, b_dst, h_dst, slot)
            proj.norm()
            stage_kv(z_src)
            for c in range(n_chunks):
                ch = Chunk(z_src, b_src, c * CHUNK, c, exact=False,
                           masked=(not has_cache) and c < WINDOW // CHUNK)
                ch.pre()
                proj.zproj()
                ch.mm1()
                if c == 0:
                    proj.low_rank()
                proj.zproj(2)
                if c == min(1, n_chunks - 1):
                    proj.log_decay()
                ch.post1()
                ch.mm2()
                ch.post2()
            proj.rest()
            proj.cumulate()
            out_proj(h_src)

        @pl.when(jnp.logical_not(safe))
        def _exact():
            Proj(z_dst, b_dst, h_dst, slot).all_rolled()
            stage_kv(z_src)

            def chunk_body(c, carry):
                ch = Chunk(z_src, b_src, pl.multiple_of(c * CHUNK, CHUNK), c, exact=True, masked=not has_cache)
                ch.pre()
                ch.mm1()
                ch.post1()
                ch.mm2()
                ch.post2()
                return carry

            lax.fori_loop(0, n_chunks, chunk_body, 0)
            out_proj(h_src, rolled=True)

    parity = lax.rem(g, 2)
    pl.when(parity == 0)(functools.partial(step, 0))
    pl.when(parity == 1)(functools.partial(step, 1))

    k_tail = kbuf_ref[t_tile:t_tile + WINDOW, :]
    v_tail = vbuf_ref[t_tile:t_tile + WINDOW, :]
    kbuf_ref[0:WINDOW, :] = k_tail
    vbuf_ref[0:WINDOW, :] = v_tail

    @pl.when(t_mix == n_t - 1)
    def _emit():
        knew_ref[0] = k_tail
        vnew_ref[0] = v_tail
        sfin_ref[0] = s_ref[...]


def _mixer_call(h, cache, gmix, win, winb, walr, wa2, ba, ggla, sinks, wout, layer, *, dk, dv, hd):
    bsz, seq, d = h.shape
    t_tile = min(seq, 256)
    assert seq % t_tile == 0 and t_tile % CHUNK == 0
    n_t = seq // t_tile
    n_tiles = bsz * n_t
    has_cache = cache is not None
    kvw = SWA_KV_HEADS * hd
    gk = GLA_HEADS * dk
    n_a = 4 * d + 2 * gk
    z_w = n_a + winb.shape[2]
    assert n_a % (2 * MXU_N) == 0 and winb.shape[2] % (2 * MXU_N) == 0
    assert GLA_HEADS * dv == d and SWA_HEADS * hd == d and SWA_GROUP * hd == dv

    def proj_tile(g):
        q = jnp.minimum(g, n_tiles - 1)
        return (q // n_t, q % n_t, 0)

    def mix_tile(g):
        q = jnp.maximum(g - 1, 0)
        return (q // n_t, q % n_t, 0)

    per_seq = lambda *shape: pl.BlockSpec((1,) + shape, lambda g: (jnp.maximum(g - 1, 0) // n_t,) + (0,) * len(shape))
    args, specs = [h], [pl.BlockSpec((1, t_tile, d), proj_tile)]
    if has_cache:
        ck, cv, s0 = cache
        args += [ck, cv, s0]
        specs += [per_seq(WINDOW, kvw), per_seq(WINDOW, kvw), per_seq(GLA_HEADS, dk, dv)]
    args += [gmix.reshape(1, d), win, winb, walr, wa2, ba.reshape(1, gk), ggla.reshape(1, d), sinks, wout]
    specs += [_const_spec((1, d)), _layer_spec(win, layer, cols=n_a), _layer_spec(winb, layer),
              _layer_spec(walr, layer), _layer_spec(wa2, layer),
              _const_spec((1, gk)), _const_spec((1, d)), pl.BlockSpec(memory_space=pltpu.SMEM),
              _layer_spec(wout, layer)]
    out_shape = (jax.ShapeDtypeStruct((bsz, seq, d), F32),
                 jax.ShapeDtypeStruct((bsz, WINDOW, kvw), F32),
                 jax.ShapeDtypeStruct((bsz, WINDOW, kvw), F32),
                 jax.ShapeDtypeStruct((bsz, GLA_HEADS, dk, dv), F32))
    out_specs = (pl.BlockSpec((1, t_tile, d), mix_tile), per_seq(WINDOW, kvw), per_seq(WINDOW, kvw),
                 per_seq(GLA_HEADS, dk, dv))
    scratch = [pltpu.VMEM((t_tile, d), BF16)]
    scratch += [pltpu.VMEM((t_tile, z_w), F32)] * 2
    scratch += [pltpu.VMEM((t_tile, gk), F32)] * 2
    scratch += [pltpu.VMEM((t_tile, d), F32)] * 2
    scratch += [
        pltpu.VMEM((t_tile, d), BF16),
        pltpu.VMEM((WINDOW + t_tile, kvw), F32),
        pltpu.VMEM((WINDOW + t_tile, kvw), F32),
        pltpu.VMEM((GLA_HEADS, dk, dv), F32),
        pltpu.SMEM((2,), jnp.int32),
    ]
    return pl.pallas_call(
        functools.partial(_mixer_kernel, t_tile=t_tile, n_t=n_t, d=d, dk=dk, dv=dv, hd=hd, has_cache=has_cache),
        out_shape=out_shape,
        grid=(n_tiles + 1,),
        in_specs=specs,
        out_specs=out_specs,
        scratch_shapes=scratch,
        compiler_params=pltpu.CompilerParams(
            dimension_semantics=("arbitrary",), vmem_limit_bytes=VMEM_LIMIT_BYTES),
        name="mixer_cache" if has_cache else "mixer",
    )(*args)


def _prep_weights(d, dk, w_ffn1_up, w_ffn1_down, w_in, w_gla_a2, w_out, w_ffn2_up, w_ffn2_down, w_ple_gate, w_ple):
    n_a = 4 * d + 2 * GLA_HEADS * dk
    return dict(
        up1=w_ffn1_up.astype(BF16), dn1=w_ffn1_down.astype(BF16),
        up2=w_ffn2_up.astype(BF16), dn2=w_ffn2_down.astype(BF16),
        win=w_in.astype(BF16), winb=w_in[:, :, n_a + GLA_RANK:].astype(BF16),
        walr=jnp.pad(w_in[:, :, n_a:n_a + GLA_RANK], ((0, 0), (0, 0), (0, LANES - GLA_RANK))).astype(BF16),
        wa2=jnp.pad(w_gla_a2, ((0, 0), (0, LANES - GLA_RANK), (0, 0))).astype(BF16),
        wout=w_out.astype(BF16), wpg=w_ple_gate.astype(BF16), wpl=w_ple.astype(BF16))


def kernel(x_prompt, x_sample, p_prompt, p_sample, cache_swa_k, cache_swa_v, state_gla, g_ffn1, w_ffn1_up, w_ffn1_down, g_mix, w_in, w_gla_a2, b_gla_a, g_gla, swa_sinks, w_out, g_ffn2, w_ffn2_up, w_ffn2_down, g_ple, w_ple_gate, w_ple, g_final):
    depth = w_in.shape[0]
    bp, sp, d = x_prompt.shape
    bs, ss, _ = x_sample.shape
    dk = state_gla.shape[-2]
    dv = state_gla.shape[-1]
    hd = cache_swa_k.shape[-1]
    kvw = SWA_KV_HEADS * hd
    wc = cache_swa_k.shape[2]
    assert wc == WINDOW and sp >= WINDOW

    xp = x_prompt.reshape(bp * sp, d)
    xs = x_sample.reshape(bs * ss, d)
    pe_p = p_prompt.reshape(depth, bp * sp, -1)
    pe_s = p_sample.reshape(depth, bs * ss, -1)
    w = _prep_weights(d, dk, w_ffn1_up, w_ffn1_down, w_in, w_gla_a2, w_out, w_ffn2_up, w_ffn2_down, w_ple_gate, w_ple)
    outs = [[] for _ in range(6)]
    for i in range(depth):
        last = g_final if i == depth - 1 else None
        mix = functools.partial(_mixer_call, gmix=g_mix[i], win=w["win"], winb=w["winb"], walr=w["walr"], wa2=w["wa2"],
                                ba=b_gla_a[i], ggla=g_gla[i], sinks=swa_sinks[i], wout=w["wout"], layer=i,
                                dk=dk, dv=dv, hd=hd)
        xp = _ffn_call(xp, g_ffn1[i], w["up1"], w["dn1"], i)
        xp3, pk, pv, ps = mix(xp.reshape(bp, sp, d), None)
        xp = _ffn_call(xp3.reshape(bp * sp, d), g_ffn2[i], w["up2"], w["dn2"], i,
                       ple=(pe_p, g_ple[i], w["wpg"], w["wpl"]), g_final=last)
        xs = _ffn_call(xs, g_ffn1[i], w["up1"], w["dn1"], i)
        cache = (cache_swa_k[i].reshape(bs, wc, kvw), cache_swa_v[i].reshape(bs, wc, kvw), state_gla[i])
        xs3, sk, sv, s_s = mix(xs.reshape(bs, ss, d), cache)
        xs = _ffn_call(xs3.reshape(bs * ss, d), g_ffn2[i], w["up2"], w["dn2"], i,
                       ple=(pe_s, g_ple[i], w["wpg"], w["wpl"]), g_final=last)
        for lst, val in zip(outs, (pk.reshape(bp, WINDOW, SWA_KV_HEADS, hd), pv.reshape(bp, WINDOW, SWA_KV_HEADS, hd),
                                   ps, sk.reshape(bs, wc, SWA_KV_HEADS, hd), sv.reshape(bs, wc, SWA_KV_HEADS, hd), s_s)):
            lst.append(val)
    return (xp.reshape(bp, sp, d), xs.reshape(bs, ss, d)) + tuple(jnp.stack(o) for o in outs)
```

```python
import functools

import jax
import jax.numpy as jnp
from jax import lax
from jax.experimental import pallas as pl
from jax.experimental.pallas import tpu as pltpu

F32 = jnp.float32
BF16 = jnp.bfloat16

CHUNK = 64
EPS = 1e-6
GLA_HEADS = 4
GLA_RANK = 16
GLA_TAU = 16.0
SWA_HEADS = 16
SWA_KV_HEADS = 4
SWA_GROUP = SWA_HEADS // SWA_KV_HEADS
WINDOW = 128
BAND = WINDOW + CHUNK

LANES = 128
MXU_N = 256
VMEM_LIMIT_BYTES = 56 * 1024 * 1024

GLA_SAFE_LOG_DECAY = -60.0
LOG2E = 1.4426950408889634


def _sigmoid(x):
    return 1.0 / (1.0 + jnp.exp(-x))


def _log_sigmoid(x):
    return jnp.minimum(x, 0.0) - jnp.log(1.0 + jnp.exp(-jnp.abs(x)))


def _rms_rows(x, g):
    ms = jnp.mean(x * x, axis=-1, keepdims=True)
    return x * lax.rsqrt(ms + EPS) * g


def _dot(a, b):
    return jnp.dot(a, b, preferred_element_type=F32)


def _dot_nt(a, b):
    return lax.dot_general(a, b, (((1,), (1,)), ((), ())), preferred_element_type=F32)


def _norm_rows_to(src_ref, g_ref, dst_ref, rows, rb):
    g = g_ref[...]
    for r in range(0, rows, rb):
        dst_ref[r:r + rb, :] = _rms_rows(src_ref[r:r + rb, :], g).astype(dst_ref.dtype)


def _ffn_kernel(*refs, tm, d_ff, has_ple, has_final):
    it = iter(refs)
    x_ref, g_ref, wup_ref, wdn_ref = next(it), next(it), next(it), next(it)
    if has_ple:
        pe_ref, gple_ref, wpg_ref, wpl_ref = next(it), next(it), next(it), next(it)
    if has_final:
        gfin_ref = next(it)
    o_ref = next(it)
    xn_ref, act_ref = next(it), next(it)
    if has_ple:
        h_ref = next(it)
    d = x_ref.shape[1]
    rb = min(tm, 64)

    _norm_rows_to(x_ref, g_ref, xn_ref, tm, rb)

    for c0 in range(0, d_ff, MXU_N):
        xn = xn_ref[...]
        gate = _dot(xn, wup_ref[:, c0:c0 + MXU_N])
        up = _dot(xn, wup_ref[:, d_ff + c0:d_ff + c0 + MXU_N])
        act_ref[:, c0:c0 + MXU_N] = (gate * _sigmoid(gate) * up).astype(BF16)

    dst_ref = h_ref if has_ple else o_ref
    for n0 in range(0, d, MXU_N):
        y = _dot(act_ref[...], wdn_ref[:, n0:n0 + MXU_N])
        dst_ref[:, n0:n0 + MXU_N] = x_ref[:, n0:n0 + MXU_N] + 0.5 * y

    if has_ple:
        _norm_rows_to(h_ref, gple_ref, xn_ref, tm, rb)
        for n0 in range(0, d, MXU_N):
            gate = _sigmoid(_dot(xn_ref[...], wpg_ref[:, n0:n0 + MXU_N]))
            emb = _dot(pe_ref[...].astype(BF16), wpl_ref[:, n0:n0 + MXU_N])
            o_ref[:, n0:n0 + MXU_N] = h_ref[:, n0:n0 + MXU_N] + gate * emb

    if has_final:
        _norm_rows_to(o_ref, gfin_ref, o_ref, tm, rb)


def _const_spec(shape):
    zeros = (0,) * len(shape)
    return pl.BlockSpec(shape, lambda *_: zeros, pipeline_mode=pl.Buffered(1))


def _layer_spec(stacked, layer, cols=None):
    _, rows, width = stacked.shape
    return pl.BlockSpec((None, rows, width if cols is None else cols), lambda *_: (layer, 0, 0),
                        pipeline_mode=pl.Buffered(1))


def _ffn_call(x, g, wup, wdn, layer, ple=None, g_final=None):
    n, d = x.shape
    d_ff = wdn.shape[1]
    tm = min(n, 1024)
    assert n % tm == 0 and d_ff % MXU_N == 0 and d % MXU_N == 0
    has_ple, has_final = ple is not None, g_final is not None
    row_spec = lambda w: pl.BlockSpec((tm, w), lambda i: (i, 0))
    args = [x, g.reshape(1, d), wup, wdn]
    specs = [row_spec(d), _const_spec((1, d)), _layer_spec(wup, layer), _layer_spec(wdn, layer)]
    scratch = [pltpu.VMEM((tm, d), BF16), pltpu.VMEM((tm, d_ff), BF16)]
    if has_ple:
        pe, gple, wpg, wpl = ple
        args += [pe, gple.reshape(1, d), wpg, wpl]
        specs += [pl.BlockSpec((None, tm, pe.shape[2]), lambda i: (layer, i, 0)), _const_spec((1, d)),
                  _layer_spec(wpg, layer), _layer_spec(wpl, layer)]
        scratch.append(pltpu.VMEM((tm, d), F32))
    if has_final:
        args.append(g_final.reshape(1, d))
        specs.append(_const_spec((1, d)))
    return pl.pallas_call(
        functools.partial(_ffn_kernel, tm=tm, d_ff=d_ff, has_ple=has_ple, has_final=has_final),
        out_shape=jax.ShapeDtypeStruct((n, d), F32),
        grid=(n // tm,),
        in_specs=specs,
        out_specs=row_spec(d),
        scratch_shapes=scratch,
        compiler_params=pltpu.CompilerParams(
            dimension_semantics=("arbitrary",), vmem_limit_bytes=VMEM_LIMIT_BYTES),
        name="ffn_ple" if has_ple else "ffn",
    )(*args)


def _mixer_kernel(*refs, t_tile, n_t, d, dk, dv, hd, has_cache):
    it = iter(refs)
    h_ref = next(it)
    if has_cache:
        ck_ref, cv_ref, s0_ref = next(it), next(it), next(it)
    gmix_ref, wina_ref, winb_ref, walr_ref, wa2_ref, ba_ref, ggla_ref, sink_ref, wout_ref = (
        next(it) for _ in range(9))
    o_ref, knew_ref, vnew_ref, sfin_ref = next(it), next(it), next(it), next(it)
    un_ref = next(it)
    z_refs, b_refs, h_refs = (next(it), next(it)), (next(it), next(it)), (next(it), next(it))
    mg_ref, kbuf_ref, vbuf_ref, s_ref, flag_ref = (next(it) for _ in range(5))

    g = pl.program_id(0)
    p = jnp.maximum(g - 1, 0)
    t_mix = lax.rem(p, n_t)
    gk = GLA_HEADS * dk
    kvw = SWA_KV_HEADS * hd
    gw = SWA_GROUP * hd
    n_chunks = t_tile // CHUNK
    c_ga, c_gb = 0, d
    c_qa = 2 * d
    c_ka = c_qa + gk
    c_va = c_ka + gk
    c_ra = c_va + d
    c_qb = c_ra + d
    c_kb = c_qb + d
    c_vb = c_kb + kvw
    z_w = c_vb + kvw
    zc = 2 * MXU_N
    rb = min(t_tile, 64)

    row = lax.broadcasted_iota(jnp.int32, (CHUNK, CHUNK), 0)
    col = lax.broadcasted_iota(jnp.int32, (CHUNK, CHUNK), 1)
    causal = row >= col
    tri = jnp.where(causal, 1.0, 0.0).astype(BF16)

    @pl.when(g == 0)
    def _first():
        z_refs[1][...] = jnp.zeros(z_refs[1].shape, F32)
        b_refs[1][...] = jnp.zeros(b_refs[1].shape, F32)
        h_refs[1][...] = jnp.zeros(h_refs[1].shape, F32)
        flag_ref[1] = 1

    @pl.when(t_mix == 0)
    def _init():
        if has_cache:
            s_ref[...] = s0_ref[0]
            kbuf_ref[0:WINDOW, :] = ck_ref[0]
            vbuf_ref[0:WINDOW, :] = cv_ref[0]
        else:
            s_ref[...] = jnp.zeros(s_ref.shape, F32)
            kbuf_ref[0:WINDOW, :] = jnp.zeros((WINDOW, kvw), F32)
            vbuf_ref[0:WINDOW, :] = jnp.zeros((WINDOW, kvw), F32)

    class Proj:
        def __init__(self, z_dst, b_dst, h_dst, flag_idx):
            self.z_dst, self.b_dst, self.h_dst, self.flag_idx = z_dst, b_dst, h_dst, flag_idx
            self.z_cols = list(range(0, z_w, zc))

        def norm(self):
            g_mix = gmix_ref[...]
            for r in range(0, t_tile, rb):
                x = h_ref[0, r:r + rb, :]
                self.h_dst[r:r + rb, :] = x
                un_ref[r:r + rb, :] = _rms_rows(x, g_mix).astype(BF16)

        def zproj(self, n=1):
            for _ in range(n):
                if self.z_cols:
                    c0 = self.z_cols.pop(0)
                    w = wina_ref[:, c0:c0 + zc] if c0 < c_qb else winb_ref[:, c0 - c_qb:c0 - c_qb + zc]
                    self.z_dst[:, c0:c0 + zc] = _dot(un_ref[...], w)

        def low_rank(self):
            self.alr = _dot(un_ref[...], walr_ref[...]).astype(BF16)

        def log_decay(self):
            self.la = _log_sigmoid(_dot(self.alr, wa2_ref[...]) + ba_ref[...]) * (1.0 / GLA_TAU)

        def cumulate(self):
            lows = []
            for r in range(0, t_tile, CHUNK):
                la_c = self.la[r:r + CHUNK, :]
                hi = la_c.astype(BF16)
                lo = (la_c - hi.astype(F32)).astype(BF16)
                b = _dot(tri, hi) + _dot(tri, lo)
                self.b_dst[r:r + CHUNK, :] = b
                lows.append(b[CHUNK - 1:CHUNK, :])
            low = functools.reduce(jnp.minimum, lows)
            flag_ref[self.flag_idx] = (jnp.min(low) >= GLA_SAFE_LOG_DECAY).astype(jnp.int32)

        def rest(self):
            self.zproj(len(self.z_cols))

        def all_rolled(self):
            g_mix = gmix_ref[...]

            def norm_body(i, carry):
                r = pl.multiple_of(i * rb, rb)
                x = h_ref[0, pl.ds(r, rb), :]
                self.h_dst[pl.ds(r, rb), :] = x
                un_ref[pl.ds(r, rb), :] = _rms_rows(x, g_mix).astype(BF16)
                return carry

            lax.fori_loop(0, t_tile // rb, norm_body, 0)
            for w_ref, base in ((wina_ref, 0), (winb_ref, c_qb)):
                def z_body(i, carry, w_ref=w_ref, base=base):
                    c0 = pl.multiple_of(i * zc, zc)
                    self.z_dst[:, pl.ds(base + c0, zc)] = _dot(un_ref[...], w_ref[:, pl.ds(c0, zc)])
                    return carry

                lax.fori_loop(0, w_ref.shape[1] // zc, z_body, 0)
            self.low_rank()
            self.log_decay()
            self.b_dst[...] = self.la

            def cum_body(c, low):
                rows = pl.ds(pl.multiple_of(c * CHUNK, CHUNK), CHUNK)
                la_c = self.b_dst[rows, :]
                hi = la_c.astype(BF16)
                b = _dot(tri, hi) + _dot(tri, (la_c - hi.astype(F32)).astype(BF16))
                self.b_dst[rows, :] = b
                return jnp.minimum(low, b[CHUNK - 1:CHUNK, :])

            low = lax.fori_loop(0, n_chunks, cum_body, jnp.zeros((1, gk), F32))
            flag_ref[self.flag_idx] = (jnp.min(low) >= GLA_SAFE_LOG_DECAY).astype(jnp.int32)

    class Chunk:
        def __init__(self, z_src, b_src, r0, c, exact, masked):
            self.z, self.b_src, self.r0, self.c, self.exact, self.masked = z_src, b_src, r0, c, exact, masked
            self.rows = pl.ds(r0, CHUNK)

        def pre(self):
            z, rows = self.z, self.rows
            b = self.b_src[rows, :]
            b_last = b[CHUNK - 1:CHUNK, :]
            q = z[rows, c_qa:c_qa + gk] * (dk ** -0.5)
            k = z[rows, c_ka:c_ka + gk]
            qg = q * jnp.exp(b)
            kdec = k * jnp.exp(b_last - b)
            a_last = jnp.exp(b_last)
            self.b, self.q, self.qgb = b, q, qg.astype(BF16)
            if not self.exact:
                self.kt = (k * jnp.exp(-b)).astype(BF16)
            pad = jnp.zeros((LANES - CHUNK - 8, dk), F32)
            self.kdT, self.acol, self.v = [], [], []
            for hh in range(GLA_HEADS):
                stacked = jnp.concatenate(
                    [kdec[:, hh * dk:(hh + 1) * dk],
                     jnp.broadcast_to(a_last[:, hh * dk:(hh + 1) * dk], (8, dk)), pad], axis=0)
                st = stacked.T
                self.kdT.append(st[:, 0:CHUNK].astype(BF16))
                self.acol.append(st[:, CHUNK:CHUNK + 1])
                self.v.append(z[rows, c_va + hh * dv:c_va + (hh + 1) * dv].astype(BF16))
            band = pl.ds(self.r0, BAND)
            self.kb, self.vb, self.qs = [], [], []
            for grp in range(SWA_KV_HEADS):
                self.kb.append(kbuf_ref[band, grp * hd:(grp + 1) * hd].astype(BF16))
                self.vb.append(vbuf_ref[band, grp * hd:(grp + 1) * hd].astype(BF16))
                qgrp = z[rows, c_qb + grp * gw:c_qb + (grp + 1) * gw] * (hd ** -0.5 * LOG2E)
                qs = jnp.concatenate([qgrp[:, m * hd:(m + 1) * hd] for m in range(SWA_GROUP)], axis=0)
                self.qs.append(qs.astype(BF16))

        def mm1(self):
            if self.exact:
                q, b, r0, z = self.q, self.b, self.r0, self.z

                def key_body(j, accs):
                    bj = self.b_src[pl.ds(r0 + j, 1), :]
                    kj = z[pl.ds(r0 + j, 1), c_ka:c_ka + gk]
                    w = q * jnp.exp(jnp.minimum(b - bj, 0.0)) * kj
                    out = []
                    for hh in range(GLA_HEADS):
                        colsum = jnp.sum(w[:, hh * dk:(hh + 1) * dk], axis=1, keepdims=True)
                        out.append(jnp.where((col == j) & causal, colsum, accs[hh]))
                    return tuple(out)

                self.scores = lax.fori_loop(0, CHUNK, key_body,
                                            tuple(jnp.zeros((CHUNK, CHUNK), F32) for _ in range(GLA_HEADS)))
            else:
                self.scores = [_dot_nt(self.qgb[:, hh * dk:(hh + 1) * dk], self.kt[:, hh * dk:(hh + 1) * dk])
                               for hh in range(GLA_HEADS)]
            self.s_qk = [_dot_nt(self.qs[grp], self.kb[grp]) for grp in range(SWA_KV_HEADS)]
            self.o_inter = []
            for hh in range(GLA_HEADS):
                s_old = s_ref[hh]
                self.o_inter.append(_dot(self.qgb[:, hh * dk:(hh + 1) * dk], s_old.astype(BF16)))
                s_ref[hh] = s_old * self.acol[hh] + _dot(self.kdT[hh], self.v[hh])

        def post1(self):
            if not self.exact:
                self.scores = [jnp.where(causal, s, 0.0) for s in self.scores]
            self.scores = [s.astype(BF16) for s in self.scores]
            if self.masked:
                cg = t_mix * n_chunks + self.c
                kcol = lax.broadcasted_iota(jnp.int32, (CHUNK, BAND), 1)
                valid = kcol >= WINDOW - CHUNK * jnp.minimum(cg, WINDOW // CHUNK)
            self.p, self.den = [], []
            for grp in range(SWA_KV_HEADS):
                ps, dens = [], []
                for m in range(SWA_GROUP):
                    sm = self.s_qk[grp][m * CHUNK:(m + 1) * CHUNK, :]
                    if self.masked:
                        sm = jnp.where(valid, sm, -jnp.inf)
                    sink = sink_ref[grp * SWA_GROUP + m] * LOG2E
                    mx = jnp.maximum(jnp.max(sm, axis=1, keepdims=True), sink)
                    pm = jnp.exp2(sm - mx)
                    dens.append(jnp.sum(pm, axis=1, keepdims=True) + jnp.exp2(sink - mx))
                    ps.append(pm.astype(BF16))
                self.p.append(jnp.concatenate(ps, axis=0))
                self.den.append(dens)

        def mm2(self):
            self.o = [_dot(self.scores[hh], self.v[hh]) + self.o_inter[hh] for hh in range(GLA_HEADS)]
            self.og = [_dot(self.p[grp], self.vb[grp]) for grp in range(SWA_KV_HEADS)]

        def post2(self):
            z, rows = self.z, self.rows
            g_gla = ggla_ref[...]
            for j in range(GLA_HEADS):
                o = _rms_rows(self.o[j], g_gla[:, j * dv:(j + 1) * dv])
                ra = z[rows, c_ra + j * dv:c_ra + (j + 1) * dv]
                ga = z[rows, c_ga + j * dv:c_ga + (j + 1) * dv]
                gb = z[rows, c_gb + j * gw:c_gb + (j + 1) * gw]
                ob = jnp.concatenate([self.og[j][m * CHUNK:(m + 1) * CHUNK, :] / self.den[j][m]
                                      for m in range(SWA_GROUP)], axis=1)
                mg_ref[rows, j * dv:(j + 1) * dv] = (
                    _sigmoid(ga) * (o * (ra * _sigmoid(ra))) + _sigmoid(gb) * ob).astype(BF16)

    def stage_kv(z_src):
        kbuf_ref[WINDOW:WINDOW + t_tile, :] = z_src[:, c_kb:c_kb + kvw]
        vbuf_ref[WINDOW:WINDOW + t_tile, :] = z_src[:, c_vb:c_vb + kvw]

    def out_proj(h_src, rolled=False):
        def body(n, carry):
            cols = pl.ds(n * MXU_N, MXU_N) if isinstance(n, int) else pl.ds(pl.multiple_of(n * MXU_N, MXU_N), MXU_N)
            o_ref[0, :, cols] = h_src[:, cols] + _dot(mg_ref[...], wout_ref[:, cols])
            return carry

        if rolled:
            lax.fori_loop(0, d // MXU_N, body, 0)
        else:
            for n in range(d // MXU_N):
                body(n, 0)

    def step(slot):
        z_dst, b_dst, h_dst = z_refs[slot], b_refs[slot], h_refs[slot]
        z_src, b_src, h_src = z_refs[1 - slot], b_refs[1 - slot], h_refs[1 - slot]
        safe = flag_ref[1 - slot] == 1

        @pl.when(safe)
        def _fused():
            proj = Proj(z_dst, b_dst, h_dst, slot)
            proj.norm()
            stage_kv(z_src)
            for c in range(n_chunks):
                ch = Chunk(z_src, b_src, c * CHUNK, c, exact=False,
                           masked=(not has_cache) and c < WINDOW // CHUNK)
                ch.pre()
                proj.zproj()
                ch.mm1()
                if c == 0:
                    proj.low_rank()
                proj.zproj(2)
                if c == min(1, n_chunks - 1):
                    proj.log_decay()
                ch.post1()
                ch.mm2()
                ch.post2()
            proj.rest()
            proj.cumulate()
            out_proj(h_src)

        @pl.when(jnp.logical_not(safe))
        def _exact():
            Proj(z_dst, b_dst, h_dst, slot).all_rolled()
            stage_kv(z_src)

            def chunk_body(c, carry):
                ch = Chunk(z_src, b_src, pl.multiple_of(c * CHUNK, CHUNK), c, exact=True, masked=not has_cache)
                ch.pre()
                ch.mm1()
                ch.post1()
                ch.mm2()
                ch.post2()
                return carry

            lax.fori_loop(0, n_chunks, chunk_body, 0)
            out_proj(h_src, rolled=True)

    parity = lax.rem(g, 2)
    pl.when(parity == 0)(functools.partial(step, 0))
    pl.when(parity == 1)(functools.partial(step, 1))

    k_tail = kbuf_ref[t_tile:t_tile + WINDOW, :]
    v_tail = vbuf_ref[t_tile:t_tile + WINDOW, :]
    kbuf_ref[0:WINDOW, :] = k_tail
    vbuf_ref[0:WINDOW, :] = v_tail

    @pl.when(t_mix == n_t - 1)
    def _emit():
        knew_ref[0] = k_tail
        vnew_ref[0] = v_tail
        sfin_ref[0] = s_ref[...]


def _mixer_call(h, cache, gmix, win, winb, walr, wa2, ba, ggla, sinks, wout, layer, *, dk, dv, hd):
    bsz, seq, d = h.shape
    t_tile = min(seq, 256)
    assert seq % t_tile == 0 and t_tile % CHUNK == 0
    n_t = seq // t_tile
    n_tiles = bsz * n_t
    has_cache = cache is not None
    kvw = SWA_KV_HEADS * hd
    gk = GLA_HEADS * dk
    n_a = 4 * d + 2 * gk
    z_w = n_a + winb.shape[2]
    assert n_a % (2 * MXU_N) == 0 and winb.shape[2] % (2 * MXU_N) == 0
    assert GLA_HEADS * dv == d and SWA_HEADS * hd == d and SWA_GROUP * hd == dv

    def proj_tile(g):
        q = jnp.minimum(g, n_tiles - 1)
        return (q // n_t, q % n_t, 0)

    def mix_tile(g):
        q = jnp.maximum(g - 1, 0)
        return (q // n_t, q % n_t, 0)

    per_seq = lambda *shape: pl.BlockSpec((1,) + shape, lambda g: (jnp.maximum(g - 1, 0) // n_t,) + (0,) * len(shape))
    args, specs = [h], [pl.BlockSpec((1, t_tile, d), proj_tile)]
    if has_cache:
        ck, cv, s0 = cache
        args += [ck, cv, s0]
        specs += [per_seq(WINDOW, kvw), per_seq(WINDOW, kvw), per_seq(GLA_HEADS, dk, dv)]
    args += [gmix.reshape(1, d), win, winb, walr, wa2, ba.reshape(1, gk), ggla.reshape(1, d), sinks, wout]
    specs += [_const_spec((1, d)), _layer_spec(win, layer, cols=n_a), _layer_spec(winb, layer),
              _layer_spec(walr, layer), _layer_spec(wa2, layer),
              _const_spec((1, gk)), _const_spec((1, d)), pl.BlockSpec(memory_space=pltpu.SMEM),
              _layer_spec(wout, layer)]
    out_shape = (jax.ShapeDtypeStruct((bsz, seq, d), F32),
                 jax.ShapeDtypeStruct((bsz, WINDOW, kvw), F32),
                 jax.ShapeDtypeStruct((bsz, WINDOW, kvw), F32),
                 jax.ShapeDtypeStruct((bsz, GLA_HEADS, dk, dv), F32))
    out_specs = (pl.BlockSpec((1, t_tile, d), mix_tile), per_seq(WINDOW, kvw), per_seq(WINDOW, kvw),
                 per_seq(GLA_HEADS, dk, dv))
    scratch = [pltpu.VMEM((t_tile, d), BF16)]
    scratch += [pltpu.VMEM((t_tile, z_w), F32)] * 2
    scratch += [pltpu.VMEM((t_tile, gk), F32)] * 2
    scratch += [pltpu.VMEM((t_tile, d), F32)] * 2
    scratch += [
        pltpu.VMEM((t_tile, d), BF16),
        pltpu.VMEM((WINDOW + t_tile, kvw), F32),
        pltpu.VMEM((WINDOW + t_tile, kvw), F32),
        pltpu.VMEM((GLA_HEADS, dk, dv), F32),
        pltpu.SMEM((2,), jnp.int32),
    ]
    return pl.pallas_call(
        functools.partial(_mixer_kernel, t_tile=t_tile, n_t=n_t, d=d, dk=dk, dv=dv, hd=hd, has_cache=has_cache),
        out_shape=out_shape,
        grid=(n_tiles + 1,),
        in_specs=specs,
        out_specs=out_specs,
        scratch_shapes=scratch,
        compiler_params=pltpu.CompilerParams(
            dimension_semantics=("arbitrary",), vmem_limit_bytes=VMEM_LIMIT_BYTES),
        name="mixer_cache" if has_cache else "mixer",
    )(*args)


def _prep_weights(d, dk, w_ffn1_up, w_ffn1_down, w_in, w_gla_a2, w_out, w_ffn2_up, w_ffn2_down, w_ple_gate, w_ple):
    n_a = 4 * d + 2 * GLA_HEADS * dk
    return dict(
        up1=w_ffn1_up.astype(BF16), dn1=w_ffn1_down.astype(BF16),
        up2=w_ffn2_up.astype(BF16), dn2=w_ffn2_down.astype(BF16),
        win=w_in.astype(BF16), winb=w_in[:, :, n_a + GLA_RANK:].astype(BF16),
        walr=jnp.pad(w_in[:, :, n_a:n_a + GLA_RANK], ((0, 0), (0, 0), (0, LANES - GLA_RANK))).astype(BF16),
        wa2=jnp.pad(w_gla_a2, ((0, 0), (0, LANES - GLA_RANK), (0, 0))).astype(BF16),
        wout=w_out.astype(BF16), wpg=w_ple_gate.astype(BF16), wpl=w_ple.astype(BF16))


def kernel(x_prompt, x_sample, p_prompt, p_sample, cache_swa_k, cache_swa_v, state_gla, g_ffn1, w_ffn1_up, w_ffn1_down, g_mix, w_in, w_gla_a2, b_gla_a, g_gla, swa_sinks, w_out, g_ffn2, w_ffn2_up, w_ffn2_down, g_ple, w_ple_gate, w_ple, g_final):
    depth = w_in.shape[0]
    bp, sp, d = x_prompt.shape
    bs, ss, _ = x_sample.shape
    dk = state_gla.shape[-2]
    dv = state_gla.shape[-1]
    hd = cache_swa_k.shape[-1]
    kvw = SWA_KV_HEADS * hd
    wc = cache_swa_k.shape[2]
    assert wc == WINDOW and sp >= WINDOW

    xp = x_prompt.reshape(bp * sp, d)
    xs = x_sample.reshape(bs * ss, d)
    pe_p = p_prompt.reshape(depth, bp * sp, -1)
    pe_s = p_sample.reshape(depth, bs * ss, -1)
    w = _prep_weights(d, dk, w_ffn1_up, w_ffn1_down, w_in, w_gla_a2, w_out, w_ffn2_up, w_ffn2_down, w_ple_gate, w_ple)
    outs = [[] for _ in range(6)]
    for i in range(depth):
        last = g_final if i == depth - 1 else None
        mix = functools.partial(_mixer_call, gmix=g_mix[i], win=w["win"], winb=w["winb"], walr=w["walr"], wa2=w["wa2"],
                                ba=b_gla_a[i], ggla=g_gla[i], sinks=swa_sinks[i], wout=w["wout"], layer=i,
                                dk=dk, dv=dv, hd=hd)
        xp = _ffn_call(xp, g_ffn1[i], w["up1"], w["dn1"], i)
        xp3, pk, pv, ps = mix(xp.reshape(bp, sp, d), None)
        xp = _ffn_call(xp3.reshape(bp * sp, d), g_ffn2[i], w["up2"], w["dn2"], i,
                       ple=(pe_p, g_ple[i], w["wpg"], w["wpl"]), g_final=last)
        xs = _ffn_call(xs, g_ffn1[i], w["up1"], w["dn1"], i)
        cache = (cache_swa_k[i].reshape(bs, wc, kvw), cache_swa_v[i].reshape(bs, wc, kvw), state_gla[i])
        xs3, sk, sv, s_s = mix(xs.reshape(bs, ss, d), cache)
        xs = _ffn_call(xs3.reshape(bs * ss, d), g_ffn2[i], w["up2"], w["dn2"], i,
                       ple=(pe_s, g_ple[i], w["wpg"], w["wpl"]), g_final=last)
        for lst, val in zip(outs, (pk.reshape(bp, WINDOW, SWA_KV_HEADS, hd), pv.reshape(bp, WINDOW, SWA_KV_HEADS, hd),
                                   ps, sk.reshape(bs, wc, SWA_KV_HEADS, hd), sv.reshape(bs, wc, SWA_KV_HEADS, hd), s_s)):
            lst.append(val)
    return (xp.reshape(bp, sp, d), xs.reshape(bs, ss, d)) + tuple(jnp.stack(o) for o in outs)
```

```python
import functools

import jax
import jax.numpy as jnp
from jax import lax
from jax.experimental import pallas as pl
from jax.experimental.pallas import tpu as pltpu

F32 = jnp.float32
BF16 = jnp.bfloat16

CHUNK = 64
EPS = 1e-6
GLA_HEADS = 4
GLA_RANK = 16
GLA_TAU = 16.0
SWA_HEADS = 16
SWA_KV_HEADS = 4
SWA_GROUP = SWA_HEADS // SWA_KV_HEADS
WINDOW = 128
BAND = WINDOW + CHUNK

LANES = 128
SUBLANES = 8
MXU_N = 256
VMEM_LIMIT_BYTES = 56 * 1024 * 1024
FFN_ROW_TILE = 1024
MIXER_ROW_TILE = 256
NORM_ROW_BLOCK = 64

GLA_SAFE_LOG_DECAY = -60.0
LOG2E = 1.4426950408889634


def _sigmoid(x):
    return 1.0 / (1.0 + jnp.exp(-x))


def _log_sigmoid(x):
    return jnp.minimum(x, 0.0) - jnp.log(1.0 + jnp.exp(-jnp.abs(x)))


def _rms_rows(x, g):
    ms = jnp.mean(x * x, axis=-1, keepdims=True)
    return x * lax.rsqrt(ms + EPS) * g


def _dot(a, b):
    return jnp.dot(a, b, preferred_element_type=F32)


def _dot_nt(a, b):
    return lax.dot_general(a, b, (((1,), (1,)), ((), ())), preferred_element_type=F32)


def _norm_rows_to(src_ref, g_ref, dst_ref, rows, rb):
    g = g_ref[...]
    for r in range(0, rows, rb):
        dst_ref[r:r + rb, :] = _rms_rows(src_ref[r:r + rb, :], g).astype(dst_ref.dtype)


def _ffn_kernel(*refs, tm, d_ff, has_ple, has_final):
    it = iter(refs)
    x_ref, g_ref, wup_ref, wdn_ref = next(it), next(it), next(it), next(it)
    if has_ple:
        pe_ref, gple_ref, wpg_ref, wpl_ref = next(it), next(it), next(it), next(it)
    if has_final:
        gfin_ref = next(it)
    o_ref = next(it)
    xn_ref, act_ref = next(it), next(it)
    if has_ple:
        h_ref = next(it)
    d = x_ref.shape[1]
    rb = min(tm, NORM_ROW_BLOCK)

    _norm_rows_to(x_ref, g_ref, xn_ref, tm, rb)

    for c0 in range(0, d_ff, MXU_N):
        xn = xn_ref[...]
        gate = _dot(xn, wup_ref[:, c0:c0 + MXU_N])
        up = _dot(xn, wup_ref[:, d_ff + c0:d_ff + c0 + MXU_N])
        act_ref[:, c0:c0 + MXU_N] = (gate * _sigmoid(gate) * up).astype(BF16)

    dst_ref = h_ref if has_ple else o_ref
    for n0 in range(0, d, MXU_N):
        y = _dot(act_ref[...], wdn_ref[:, n0:n0 + MXU_N])
        dst_ref[:, n0:n0 + MXU_N] = x_ref[:, n0:n0 + MXU_N] + 0.5 * y

    if has_ple:
        _norm_rows_to(h_ref, gple_ref, xn_ref, tm, rb)
        for n0 in range(0, d, MXU_N):
            gate = _sigmoid(_dot(xn_ref[...], wpg_ref[:, n0:n0 + MXU_N]))
            emb = _dot(pe_ref[...].astype(BF16), wpl_ref[:, n0:n0 + MXU_N])
            o_ref[:, n0:n0 + MXU_N] = h_ref[:, n0:n0 + MXU_N] + gate * emb

    if has_final:
        _norm_rows_to(o_ref, gfin_ref, o_ref, tm, rb)


def _const_spec(shape):
    zeros = (0,) * len(shape)
    return pl.BlockSpec(shape, lambda *_: zeros, pipeline_mode=pl.Buffered(1))


def _layer_spec(stacked, layer, cols=None):
    _, rows, width = stacked.shape
    return pl.BlockSpec((None, rows, width if cols is None else cols), lambda *_: (layer, 0, 0),
                        pipeline_mode=pl.Buffered(1))


def _ffn_call(x, g, wup, wdn, layer, ple=None, g_final=None):
    n, d = x.shape
    d_ff = wdn.shape[1]
    tm = min(n, FFN_ROW_TILE)
    assert n % tm == 0 and d_ff % MXU_N == 0 and d % MXU_N == 0
    has_ple, has_final = ple is not None, g_final is not None
    row_spec = lambda w: pl.BlockSpec((tm, w), lambda i: (i, 0))
    args = [x, g.reshape(1, d), wup, wdn]
    specs = [row_spec(d), _const_spec((1, d)), _layer_spec(wup, layer), _layer_spec(wdn, layer)]
    scratch = [pltpu.VMEM((tm, d), BF16), pltpu.VMEM((tm, d_ff), BF16)]
    if has_ple:
        pe, gple, wpg, wpl = ple
        args += [pe, gple.reshape(1, d), wpg, wpl]
        specs += [pl.BlockSpec((None, tm, pe.shape[2]), lambda i: (layer, i, 0)), _const_spec((1, d)),
                  _layer_spec(wpg, layer), _layer_spec(wpl, layer)]
        scratch.append(pltpu.VMEM((tm, d), F32))
    if has_final:
        args.append(g_final.reshape(1, d))
        specs.append(_const_spec((1, d)))
    return pl.pallas_call(
        functools.partial(_ffn_kernel, tm=tm, d_ff=d_ff, has_ple=has_ple, has_final=has_final),
        out_shape=jax.ShapeDtypeStruct((n, d), F32),
        grid=(n // tm,),
        in_specs=specs,
        out_specs=row_spec(d),
        scratch_shapes=scratch,
        compiler_params=pltpu.CompilerParams(
            dimension_semantics=("arbitrary",), vmem_limit_bytes=VMEM_LIMIT_BYTES),
        name="ffn_ple" if has_ple else "ffn",
    )(*args)


def _mixer_kernel(*refs, t_tile, n_t, d, dk, dv, hd, has_cache):
    it = iter(refs)
    h_ref = next(it)
    if has_cache:
        ck_ref, cv_ref, s0_ref = next(it), next(it), next(it)
    gmix_ref, wina_ref, winb_ref, walr_ref, wa2_ref, ba_ref, ggla_ref, sink_ref, wout_ref = (
        next(it) for _ in range(9))
    o_ref, knew_ref, vnew_ref, sfin_ref = next(it), next(it), next(it), next(it)
    un_ref = next(it)
    z_refs, b_refs, h_refs = (next(it), next(it)), (next(it), next(it)), (next(it), next(it))
    mg_ref, kbuf_ref, vbuf_ref, s_ref, flag_ref = (next(it) for _ in range(5))

    g = pl.program_id(0)
    p = jnp.maximum(g - 1, 0)
    t_mix = lax.rem(p, n_t)
    gk = GLA_HEADS * dk
    kvw = SWA_KV_HEADS * hd
    gw = SWA_GROUP * hd
    n_chunks = t_tile // CHUNK
    c_ga, c_gb = 0, d
    c_qa = 2 * d
    c_ka = c_qa + gk
    c_va = c_ka + gk
    c_ra = c_va + d
    c_qb = c_ra + d
    c_kb = c_qb + d
    c_vb = c_kb + kvw
    z_w = c_vb + kvw
    zc = 2 * MXU_N
    rb = min(t_tile, NORM_ROW_BLOCK)

    row = lax.broadcasted_iota(jnp.int32, (CHUNK, CHUNK), 0)
    col = lax.broadcasted_iota(jnp.int32, (CHUNK, CHUNK), 1)
    causal = row >= col
    tri = jnp.where(causal, 1.0, 0.0).astype(BF16)

    @pl.when(g == 0)
    def _first():
        z_refs[1][...] = jnp.zeros(z_refs[1].shape, F32)
        b_refs[1][...] = jnp.zeros(b_refs[1].shape, F32)
        h_refs[1][...] = jnp.zeros(h_refs[1].shape, F32)
        flag_ref[1] = 1

    @pl.when(t_mix == 0)
    def _init():
        if has_cache:
            s_ref[...] = s0_ref[0]
            kbuf_ref[0:WINDOW, :] = ck_ref[0]
            vbuf_ref[0:WINDOW, :] = cv_ref[0]
        else:
            s_ref[...] = jnp.zeros(s_ref.shape, F32)
            kbuf_ref[0:WINDOW, :] = jnp.zeros((WINDOW, kvw), F32)
            vbuf_ref[0:WINDOW, :] = jnp.zeros((WINDOW, kvw), F32)

    class Proj:
        def __init__(self, z_dst, b_dst, h_dst, flag_idx):
            self.z_dst, self.b_dst, self.h_dst, self.flag_idx = z_dst, b_dst, h_dst, flag_idx
            self.z_cols = list(range(0, z_w, zc))

        def norm(self):
            g_mix = gmix_ref[...]
            for r in range(0, t_tile, rb):
                x = h_ref[0, r:r + rb, :]
                self.h_dst[r:r + rb, :] = x
                un_ref[r:r + rb, :] = _rms_rows(x, g_mix).astype(BF16)

        def zproj(self, n=1):
            for _ in range(n):
                if self.z_cols:
                    c0 = self.z_cols.pop(0)
                    w = wina_ref[:, c0:c0 + zc] if c0 < c_qb else winb_ref[:, c0 - c_qb:c0 - c_qb + zc]
                    self.z_dst[:, c0:c0 + zc] = _dot(un_ref[...], w)

        def low_rank(self):
            self.alr = _dot(un_ref[...], walr_ref[...]).astype(BF16)

        def log_decay(self):
            self.la = _log_sigmoid(_dot(self.alr, wa2_ref[...]) + ba_ref[...]) * (1.0 / GLA_TAU)

        def cumulate(self):
            lows = []
            for r in range(0, t_tile, CHUNK):
                la_c = self.la[r:r + CHUNK, :]
                hi = la_c.astype(BF16)
                lo = (la_c - hi.astype(F32)).astype(BF16)
                b = _dot(tri, hi) + _dot(tri, lo)
                self.b_dst[r:r + CHUNK, :] = b
                lows.append(b[CHUNK - 1:CHUNK, :])
            low = functools.reduce(jnp.minimum, lows)
            flag_ref[self.flag_idx] = (jnp.min(low) >= GLA_SAFE_LOG_DECAY).astype(jnp.int32)

        def rest(self):
            self.zproj(len(self.z_cols))

        def all_rolled(self):
            g_mix = gmix_ref[...]

            def norm_body(i, carry):
                r = pl.multiple_of(i * rb, rb)
                x = h_ref[0, pl.ds(r, rb), :]
                self.h_dst[pl.ds(r, rb), :] = x
                un_ref[pl.ds(r, rb), :] = _rms_rows(x, g_mix).astype(BF16)
                return carry

            lax.fori_loop(0, t_tile // rb, norm_body, 0)
            for w_ref, base in ((wina_ref, 0), (winb_ref, c_qb)):
                def z_body(i, carry, w_ref=w_ref, base=base):
                    c0 = pl.multiple_of(i * zc, zc)
                    self.z_dst[:, pl.ds(base + c0, zc)] = _dot(un_ref[...], w_ref[:, pl.ds(c0, zc)])
                    return carry

                lax.fori_loop(0, w_ref.shape[1] // zc, z_body, 0)
            self.low_rank()
            self.log_decay()
            self.b_dst[...] = self.la

            def cum_body(c, low):
                rows = pl.ds(pl.multiple_of(c * CHUNK, CHUNK), CHUNK)
                la_c = self.b_dst[rows, :]
                hi = la_c.astype(BF16)
                b = _dot(tri, hi) + _dot(tri, (la_c - hi.astype(F32)).astype(BF16))
                self.b_dst[rows, :] = b
                return jnp.minimum(low, b[CHUNK - 1:CHUNK, :])

            low = lax.fori_loop(0, n_chunks, cum_body, jnp.zeros((1, gk), F32))
            flag_ref[self.flag_idx] = (jnp.min(low) >= GLA_SAFE_LOG_DECAY).astype(jnp.int32)

    class Chunk:
        def __init__(self, z_src, b_src, r0, c, exact, masked):
            self.z, self.b_src, self.r0, self.c, self.exact, self.masked = z_src, b_src, r0, c, exact, masked
            self.rows = pl.ds(r0, CHUNK)

        def pre(self):
            z, rows = self.z, self.rows
            b = self.b_src[rows, :]
            b_last = b[CHUNK - 1:CHUNK, :]
            q = z[rows, c_qa:c_qa + gk] * (dk ** -0.5)
            k = z[rows, c_ka:c_ka + gk]
            qg = q * jnp.exp(b)
            kdec = k * jnp.exp(b_last - b)
            a_last = jnp.exp(b_last)
            self.b, self.q, self.qgb = b, q, qg.astype(BF16)
            if not self.exact:
                self.kt = (k * jnp.exp(-b)).astype(BF16)
            pad = jnp.zeros((LANES - CHUNK - SUBLANES, dk), F32)
            self.kdT, self.acol, self.v = [], [], []
            for hh in range(GLA_HEADS):
                stacked = jnp.concatenate(
                    [kdec[:, hh * dk:(hh + 1) * dk],
                     jnp.broadcast_to(a_last[:, hh * dk:(hh + 1) * dk], (SUBLANES, dk)), pad], axis=0)
                st = stacked.T
                self.kdT.append(st[:, 0:CHUNK].astype(BF16))
                self.acol.append(st[:, CHUNK:CHUNK + 1])
                self.v.append(z[rows, c_va + hh * dv:c_va + (hh + 1) * dv].astype(BF16))
            band = pl.ds(self.r0, BAND)
            self.kb, self.vb, self.qs = [], [], []
            for grp in range(SWA_KV_HEADS):
                self.kb.append(kbuf_ref[band, grp * hd:(grp + 1) * hd].astype(BF16))
                self.vb.append(vbuf_ref[band, grp * hd:(grp + 1) * hd].astype(BF16))
                qgrp = z[rows, c_qb + grp * gw:c_qb + (grp + 1) * gw] * (hd ** -0.5 * LOG2E)
                qs = jnp.concatenate([qgrp[:, m * hd:(m + 1) * hd] for m in range(SWA_GROUP)], axis=0)
                self.qs.append(qs.astype(BF16))

        def mm1(self):
            if self.exact:
                q, b, r0, z = self.q, self.b, self.r0, self.z

                def key_body(j, accs):
                    bj = self.b_src[pl.ds(r0 + j, 1), :]
                    kj = z[pl.ds(r0 + j, 1), c_ka:c_ka + gk]
                    w = q * jnp.exp(jnp.minimum(b - bj, 0.0)) * kj
                    out = []
                    for hh in range(GLA_HEADS):
                        colsum = jnp.sum(w[:, hh * dk:(hh + 1) * dk], axis=1, keepdims=True)
                        out.append(jnp.where((col == j) & causal, colsum, accs[hh]))
                    return tuple(out)

                self.scores = lax.fori_loop(0, CHUNK, key_body,
                                            tuple(jnp.zeros((CHUNK, CHUNK), F32) for _ in range(GLA_HEADS)))
            else:
                self.scores = [_dot_nt(self.qgb[:, hh * dk:(hh + 1) * dk], self.kt[:, hh * dk:(hh + 1) * dk])
                               for hh in range(GLA_HEADS)]
            self.s_qk = [_dot_nt(self.qs[grp], self.kb[grp]) for grp in range(SWA_KV_HEADS)]
            self.o_inter = []
            for hh in range(GLA_HEADS):
                s_old = s_ref[hh]
                self.o_inter.append(_dot(self.qgb[:, hh * dk:(hh + 1) * dk], s_old.astype(BF16)))
                s_ref[hh] = s_old * self.acol[hh] + _dot(self.kdT[hh], self.v[hh])

        def post1(self):
            if not self.exact:
                self.scores = [jnp.where(causal, s, 0.0) for s in self.scores]
            self.scores = [s.astype(BF16) for s in self.scores]
            if self.masked:
                cg = t_mix * n_chunks + self.c
                kcol = lax.broadcasted_iota(jnp.int32, (CHUNK, BAND), 1)
                valid = kcol >= WINDOW - CHUNK * jnp.minimum(cg, WINDOW // CHUNK)
            self.p, self.den = [], []
            for grp in range(SWA_KV_HEADS):
                ps, dens = [], []
                for m in range(SWA_GROUP):
                    sm = self.s_qk[grp][m * CHUNK:(m + 1) * CHUNK, :]
                    if self.masked:
                        sm = jnp.where(valid, sm, -jnp.inf)
                    sink = sink_ref[grp * SWA_GROUP + m] * LOG2E
                    mx = jnp.maximum(jnp.max(sm, axis=1, keepdims=True), sink)
                    pm = jnp.exp2(sm - mx)
                    dens.append(jnp.sum(pm, axis=1, keepdims=True) + jnp.exp2(sink - mx))
                    ps.append(pm.astype(BF16))
                self.p.append(jnp.concatenate(ps, axis=0))
                self.den.append(dens)

        def mm2(self):
            self.o = [_dot(self.scores[hh], self.v[hh]) + self.o_inter[hh] for hh in range(GLA_HEADS)]
            self.og = [_dot(self.p[grp], self.vb[grp]) for grp in range(SWA_KV_HEADS)]

        def post2(self):
            z, rows = self.z, self.rows
            g_gla = ggla_ref[...]
            for j in range(GLA_HEADS):
                o = _rms_rows(self.o[j], g_gla[:, j * dv:(j + 1) * dv])
                ra = z[rows, c_ra + j * dv:c_ra + (j + 1) * dv]
                ga = z[rows, c_ga + j * dv:c_ga + (j + 1) * dv]
                gb = z[rows, c_gb + j * gw:c_gb + (j + 1) * gw]
                ob = jnp.concatenate([self.og[j][m * CHUNK:(m + 1) * CHUNK, :] / self.den[j][m]
                                      for m in range(SWA_GROUP)], axis=1)
                mg_ref[rows, j * dv:(j + 1) * dv] = (
                    _sigmoid(ga) * (o * (ra * _sigmoid(ra))) + _sigmoid(gb) * ob).astype(BF16)

    def stage_kv(z_src):
        kbuf_ref[WINDOW:WINDOW + t_tile, :] = z_src[:, c_kb:c_kb + kvw]
        vbuf_ref[WINDOW:WINDOW + t_tile, :] = z_src[:, c_vb:c_vb + kvw]

    def out_proj(h_src, rolled=False):
        def body(n, carry):
            cols = pl.ds(n * MXU_N, MXU_N) if isinstance(n, int) else pl.ds(pl.multiple_of(n * MXU_N, MXU_N), MXU_N)
            o_ref[0, :, cols] = h_src[:, cols] + _dot(mg_ref[...], wout_ref[:, cols])
            return carry

        if rolled:
            lax.fori_loop(0, d // MXU_N, body, 0)
        else:
            for n in range(d // MXU_N):
                body(n, 0)

    def step(slot):
        z_dst, b_dst, h_dst = z_refs[slot], b_refs[slot], h_refs[slot]
        z_src, b_src, h_src = z_refs[1 - slot], b_refs[1 - slot], h_refs[1 - slot]
        safe = flag_ref[1 - slot] == 1

        @pl.when(safe)
        def _fused():
            proj = Proj(z_dst, b_dst, h_dst, slot)
            stage_kv(z_src)
            for c in range(n_chunks):
                ch = Chunk(z_src, b_src, c * CHUNK, c, exact=False,
                           masked=(not has_cache) and c < WINDOW // CHUNK)
                ch.pre()
                if c == 0:
                    ch.mm1()
                    proj.norm()
                    proj.zproj()
                else:
                    proj.zproj()
                    ch.mm1()
                if c == 0:
                    proj.low_rank()
                proj.zproj(2)
                if c == min(1, n_chunks - 1):
                    proj.log_decay()
                ch.post1()
                ch.mm2()
                ch.post2()
            proj.rest()
            proj.cumulate()
            out_proj(h_src)

        @pl.when(jnp.logical_not(safe))
        def _exact():
            Proj(z_dst, b_dst, h_dst, slot).all_rolled()
            stage_kv(z_src)

            def chunk_body(c, carry):
                ch = Chunk(z_src, b_src, pl.multiple_of(c * CHUNK, CHUNK), c, exact=True, masked=not has_cache)
                ch.pre()
                ch.mm1()
                ch.post1()
                ch.mm2()
                ch.post2()
                return carry

            lax.fori_loop(0, n_chunks, chunk_body, 0)
            out_proj(h_src, rolled=True)

    parity = lax.rem(g, 2)
    pl.when(parity == 0)(functools.partial(step, 0))
    pl.when(parity == 1)(functools.partial(step, 1))

    k_tail = kbuf_ref[t_tile:t_tile + WINDOW, :]
    v_tail = vbuf_ref[t_tile:t_tile + WINDOW, :]
    kbuf_ref[0:WINDOW, :] = k_tail
    vbuf_ref[0:WINDOW, :] = v_tail

    @pl.when(t_mix == n_t - 1)
    def _emit():
        knew_ref[0] = k_tail
        vnew_ref[0] = v_tail
        sfin_ref[0] = s_ref[...]


def _mixer_call(h, cache, gmix, win, winb, walr, wa2, ba, ggla, sinks, wout, layer, *, dk, dv, hd):
    bsz, seq, d = h.shape
    t_tile = min(seq, MIXER_ROW_TILE)
    assert seq % t_tile == 0 and t_tile % CHUNK == 0
    n_t = seq // t_tile
    n_tiles = bsz * n_t
    has_cache = cache is not None
    kvw = SWA_KV_HEADS * hd
    gk = GLA_HEADS * dk
    n_a = 4 * d + 2 * gk
    z_w = n_a + winb.shape[2]
    assert n_a % (2 * MXU_N) == 0 and winb.shape[2] % (2 * MXU_N) == 0
    assert GLA_HEADS * dv == d and SWA_HEADS * hd == d and SWA_GROUP * hd == dv

    def proj_tile(g):
        q = jnp.minimum(g, n_tiles - 1)
        return (q // n_t, q % n_t, 0)

    def mix_tile(g):
        q = jnp.maximum(g - 1, 0)
        return (q // n_t, q % n_t, 0)

    per_seq = lambda *shape: pl.BlockSpec((1,) + shape, lambda g: (jnp.maximum(g - 1, 0) // n_t,) + (0,) * len(shape))
    args, specs = [h], [pl.BlockSpec((1, t_tile, d), proj_tile)]
    if has_cache:
        ck, cv, s0 = cache
        args += [ck, cv, s0]
        specs += [per_seq(WINDOW, kvw), per_seq(WINDOW, kvw), per_seq(GLA_HEADS, dk, dv)]
    args += [gmix.reshape(1, d), win, winb, walr, wa2, ba.reshape(1, gk), ggla.reshape(1, d), sinks, wout]
    specs += [_const_spec((1, d)), _layer_spec(win, layer, cols=n_a), _layer_spec(winb, layer),
              _layer_spec(walr, layer), _layer_spec(wa2, layer),
              _const_spec((1, gk)), _const_spec((1, d)), pl.BlockSpec(memory_space=pltpu.SMEM),
              _layer_spec(wout, layer)]
    out_shape = (jax.ShapeDtypeStruct((bsz, seq, d), F32),
                 jax.ShapeDtypeStruct((bsz, WINDOW, kvw), F32),
                 jax.ShapeDtypeStruct((bsz, WINDOW, kvw), F32),
                 jax.ShapeDtypeStruct((bsz, GLA_HEADS, dk, dv), F32))
    out_specs = (pl.BlockSpec((1, t_tile, d), mix_tile), per_seq(WINDOW, kvw), per_seq(WINDOW, kvw),
                 per_seq(GLA_HEADS, dk, dv))
    scratch = [pltpu.VMEM((t_tile, d), BF16)]
    scratch += [pltpu.VMEM((t_tile, z_w), F32)] * 2
    scratch += [pltpu.VMEM((t_tile, gk), F32)] * 2
    scratch += [pltpu.VMEM((t_tile, d), F32)] * 2
    scratch += [
        pltpu.VMEM((t_tile, d), BF16),
        pltpu.VMEM((WINDOW + t_tile, kvw), F32),
        pltpu.VMEM((WINDOW + t_tile, kvw), F32),
        pltpu.VMEM((GLA_HEADS, dk, dv), F32),
        pltpu.SMEM((2,), jnp.int32),
    ]
    return pl.pallas_call(
        functools.partial(_mixer_kernel, t_tile=t_tile, n_t=n_t, d=d, dk=dk, dv=dv, hd=hd, has_cache=has_cache),
        out_shape=out_shape,
        grid=(n_tiles + 1,),
        in_specs=specs,
        out_specs=out_specs,
        scratch_shapes=scratch,
        compiler_params=pltpu.CompilerParams(
            dimension_semantics=("arbitrary",), vmem_limit_bytes=VMEM_LIMIT_BYTES),
        name="mixer_cache" if has_cache else "mixer",
    )(*args)


def _prep_weights(d, dk, w_ffn1_up, w_ffn1_down, w_in, w_gla_a2, w_out, w_ffn2_up, w_ffn2_down, w_ple_gate, w_ple):
    n_a = 4 * d + 2 * GLA_HEADS * dk
    return dict(
        up1=w_ffn1_up.astype(BF16), dn1=w_ffn1_down.astype(BF16),
        up2=w_ffn2_up.astype(BF16), dn2=w_ffn2_down.astype(BF16),
        win=w_in.astype(BF16), winb=w_in[:, :, n_a + GLA_RANK:].astype(BF16),
        walr=jnp.pad(w_in[:, :, n_a:n_a + GLA_RANK], ((0, 0), (0, 0), (0, LANES - GLA_RANK))).astype(BF16),
        wa2=jnp.pad(w_gla_a2, ((0, 0), (0, LANES - GLA_RANK), (0, 0))).astype(BF16),
        wout=w_out.astype(BF16), wpg=w_ple_gate.astype(BF16), wpl=w_ple.astype(BF16))


def kernel(x_prompt, x_sample, p_prompt, p_sample, cache_swa_k, cache_swa_v, state_gla, g_ffn1, w_ffn1_up, w_ffn1_down, g_mix, w_in, w_gla_a2, b_gla_a, g_gla, swa_sinks, w_out, g_ffn2, w_ffn2_up, w_ffn2_down, g_ple, w_ple_gate, w_ple, g_final):
    depth = w_in.shape[0]
    bp, sp, d = x_prompt.shape
    bs, ss, _ = x_sample.shape
    dk = state_gla.shape[-2]
    dv = state_gla.shape[-1]
    hd = cache_swa_k.shape[-1]
    kvw = SWA_KV_HEADS * hd
    wc = cache_swa_k.shape[2]
    assert wc == WINDOW and sp >= WINDOW

    xp = x_prompt.reshape(bp * sp, d)
    xs = x_sample.reshape(bs * ss, d)
    pe_p = p_prompt.reshape(depth, bp * sp, -1)
    pe_s = p_sample.reshape(depth, bs * ss, -1)
    w = _prep_weights(d, dk, w_ffn1_up, w_ffn1_down, w_in, w_gla_a2, w_out, w_ffn2_up, w_ffn2_down, w_ple_gate, w_ple)
    outs = [[] for _ in range(6)]
    for i in range(depth):
        last = g_final if i == depth - 1 else None
        mix = functools.partial(_mixer_call, gmix=g_mix[i], win=w["win"], winb=w["winb"], walr=w["walr"], wa2=w["wa2"],
                                ba=b_gla_a[i], ggla=g_gla[i], sinks=swa_sinks[i], wout=w["wout"], layer=i,
                                dk=dk, dv=dv, hd=hd)
        xp = _ffn_call(xp, g_ffn1[i], w["up1"], w["dn1"], i)
        xp3, pk, pv, ps = mix(xp.reshape(bp, sp, d), None)
        xp = _ffn_call(xp3.reshape(bp * sp, d), g_ffn2[i], w["up2"], w["dn2"], i,
                       ple=(pe_p, g_ple[i], w["wpg"], w["wpl"]), g_final=last)
        xs = _ffn_call(xs, g_ffn1[i], w["up1"], w["dn1"], i)
        cache = (cache_swa_k[i].reshape(bs, wc, kvw), cache_swa_v[i].reshape(bs, wc, kvw), state_gla[i])
        xs3, sk, sv, s_s = mix(xs.reshape(bs, ss, d), cache)
        xs = _ffn_call(xs3.reshape(bs * ss, d), g_ffn2[i], w["up2"], w["dn2"], i,
                       ple=(pe_s, g_ple[i], w["wpg"], w["wpl"]), g_final=last)
        for lst, val in zip(outs, (pk.reshape(bp, WINDOW, SWA_KV_HEADS, hd), pv.reshape(bp, WINDOW, SWA_KV_HEADS, hd),
                                   ps, sk.reshape(bs, wc, SWA_KV_HEADS, hd), sv.reshape(bs, wc, SWA_KV_HEADS, hd), s_s)):
            lst.append(val)
    return (xp.reshape(bp, sp, d), xs.reshape(bs, ss, d)) + tuple(jnp.stack(o) for o in outs)
```

```python
import functools

import jax
import jax.numpy as jnp
from jax import lax
from jax.experimental import pallas as pl
from jax.experimental.pallas import tpu as pltpu

F32 = jnp.float32
BF16 = jnp.bfloat16

CHUNK = 64
EPS = 1e-6
GLA_HEADS = 4
GLA_RANK = 16
GLA_TAU = 16.0
SWA_HEADS = 16
SWA_KV_HEADS = 4
SWA_GROUP = SWA_HEADS // SWA_KV_HEADS
WINDOW = 128
BAND = WINDOW + CHUNK

LANES = 128
SUBLANES = 8
MXU_N = 256
VMEM_LIMIT_BYTES = 56 * 1024 * 1024
FFN_ROW_TILE = 1024
MIXER_ROW_TILE = 256
NORM_ROW_BLOCK = 64

GLA_SAFE_LOG_DECAY = -60.0
LOG2E = 1.4426950408889634


def _sigmoid(x):
    return 1.0 / (1.0 + jnp.exp(-x))


def _log_sigmoid(x):
    return jnp.minimum(x, 0.0) - jnp.log(1.0 + jnp.exp(-jnp.abs(x)))


def _rms_rows(x, g):
    ms = jnp.mean(x * x, axis=-1, keepdims=True)
    return x * lax.rsqrt(ms + EPS) * g


def _dot(a, b):
    return jnp.dot(a, b, preferred_element_type=F32)


def _dot_nt(a, b):
    return lax.dot_general(a, b, (((1,), (1,)), ((), ())), preferred_element_type=F32)


def _norm_rows_to(src_ref, g_ref, dst_ref, rows, rb):
    g = g_ref[...]
    for r in range(0, rows, rb):
        dst_ref[r:r + rb, :] = _rms_rows(src_ref[r:r + rb, :], g).astype(dst_ref.dtype)


def _ffn_kernel(*refs, tm, d_ff, has_ple, has_final):
    it = iter(refs)
    x_ref, g_ref, wup_ref, wdn_ref = next(it), next(it), next(it), next(it)
    if has_ple:
        pe_ref, gple_ref, wpg_ref, wpl_ref = next(it), next(it), next(it), next(it)
    if has_final:
        gfin_ref = next(it)
    o_ref = next(it)
    xn_ref, act_ref = next(it), next(it)
    if has_ple:
        h_ref = next(it)
    d = x_ref.shape[1]
    rb = min(tm, NORM_ROW_BLOCK)

    _norm_rows_to(x_ref, g_ref, xn_ref, tm, rb)

    for c0 in range(0, d_ff, MXU_N):
        xn = xn_ref[...]
        gate = _dot(xn, wup_ref[:, c0:c0 + MXU_N])
        up = _dot(xn, wup_ref[:, d_ff + c0:d_ff + c0 + MXU_N])
        act_ref[:, c0:c0 + MXU_N] = (gate * _sigmoid(gate) * up).astype(BF16)

    dst_ref = h_ref if has_ple else o_ref
    for n0 in range(0, d, MXU_N):
        y = _dot(act_ref[...], wdn_ref[:, n0:n0 + MXU_N])
        dst_ref[:, n0:n0 + MXU_N] = x_ref[:, n0:n0 + MXU_N] + 0.5 * y

    if has_ple:
        _norm_rows_to(h_ref, gple_ref, xn_ref, tm, rb)
        for n0 in range(0, d, MXU_N):
            gate = _sigmoid(_dot(xn_ref[...], wpg_ref[:, n0:n0 + MXU_N]))
            emb = _dot(pe_ref[...].astype(BF16), wpl_ref[:, n0:n0 + MXU_N])
            o_ref[:, n0:n0 + MXU_N] = h_ref[:, n0:n0 + MXU_N] + gate * emb

    if has_final:
        _norm_rows_to(o_ref, gfin_ref, o_ref, tm, rb)


def _const_spec(shape):
    zeros = (0,) * len(shape)
    return pl.BlockSpec(shape, lambda *_: zeros, pipeline_mode=pl.Buffered(1))


def _layer_spec(stacked, layer, cols=None):
    _, rows, width = stacked.shape
    return pl.BlockSpec((None, rows, width if cols is None else cols), lambda *_: (layer, 0, 0),
                        pipeline_mode=pl.Buffered(1))


def _ffn_call(x, g, wup, wdn, layer, ple=None, g_final=None):
    n, d = x.shape
    d_ff = wdn.shape[1]
    tm = min(n, FFN_ROW_TILE)
    assert n % tm == 0 and d_ff % MXU_N == 0 and d % MXU_N == 0
    has_ple, has_final = ple is not None, g_final is not None
    row_spec = lambda w: pl.BlockSpec((tm, w), lambda i: (i, 0))
    args = [x, g.reshape(1, d), wup, wdn]
    specs = [row_spec(d), _const_spec((1, d)), _layer_spec(wup, layer), _layer_spec(wdn, layer)]
    scratch = [pltpu.VMEM((tm, d), BF16), pltpu.VMEM((tm, d_ff), BF16)]
    if has_ple:
        pe, gple, wpg, wpl = ple
        args += [pe, gple.reshape(1, d), wpg, wpl]
        specs += [pl.BlockSpec((None, tm, pe.shape[2]), lambda i: (layer, i, 0)), _const_spec((1, d)),
                  _layer_spec(wpg, layer), _layer_spec(wpl, layer)]
        scratch.append(pltpu.VMEM((tm, d), F32))
    if has_final:
        args.append(g_final.reshape(1, d))
        specs.append(_const_spec((1, d)))
    return pl.pallas_call(
        functools.partial(_ffn_kernel, tm=tm, d_ff=d_ff, has_ple=has_ple, has_final=has_final),
        out_shape=jax.ShapeDtypeStruct((n, d), F32),
        grid=(n // tm,),
        in_specs=specs,
        out_specs=row_spec(d),
        scratch_shapes=scratch,
        compiler_params=pltpu.CompilerParams(
            dimension_semantics=("arbitrary",), vmem_limit_bytes=VMEM_LIMIT_BYTES),
        name="ffn_ple" if has_ple else "ffn",
    )(*args)


def _mixer_kernel(*refs, t_tile, n_t, d, dk, dv, hd, has_cache):
    it = iter(refs)
    h_ref = next(it)
    if has_cache:
        ck_ref, cv_ref, s0_ref = next(it), next(it), next(it)
    gmix_ref, wina_ref, winb_ref, walr_ref, wa2_ref, ba_ref, ggla_ref, sink_ref, wout_ref = (
        next(it) for _ in range(9))
    o_ref, knew_ref, vnew_ref, sfin_ref = next(it), next(it), next(it), next(it)
    un_ref = next(it)
    z_refs, b_refs, h_refs = (next(it), next(it)), (next(it), next(it)), (next(it), next(it))
    mg_ref, kbuf_ref, vbuf_ref, s_ref, flag_ref = (next(it) for _ in range(5))

    g = pl.program_id(0)
    p = jnp.maximum(g - 1, 0)
    t_mix = lax.rem(p, n_t)
    gk = GLA_HEADS * dk
    kvw = SWA_KV_HEADS * hd
    gw = SWA_GROUP * hd
    n_chunks = t_tile // CHUNK
    c_ga, c_gb = 0, d
    c_qa = 2 * d
    c_ka = c_qa + gk
    c_va = c_ka + gk
    c_ra = c_va + d
    c_qb = c_ra + d
    c_kb = c_qb + d
    c_vb = c_kb + kvw
    z_w = c_vb + kvw
    zc = 2 * MXU_N
    rb = min(t_tile, NORM_ROW_BLOCK)

    row = lax.broadcasted_iota(jnp.int32, (CHUNK, CHUNK), 0)
    col = lax.broadcasted_iota(jnp.int32, (CHUNK, CHUNK), 1)
    causal = row >= col
    tri = jnp.where(causal, 1.0, 0.0).astype(BF16)

    @pl.when(g == 0)
    def _first():
        z_refs[1][...] = jnp.zeros(z_refs[1].shape, F32)
        b_refs[1][...] = jnp.zeros(b_refs[1].shape, F32)
        h_refs[1][...] = jnp.zeros(h_refs[1].shape, F32)
        flag_ref[1] = 1

    @pl.when(t_mix == 0)
    def _init():
        if has_cache:
            s_ref[...] = s0_ref[0]
            kbuf_ref[0:WINDOW, :] = ck_ref[0]
            vbuf_ref[0:WINDOW, :] = cv_ref[0]
        else:
            s_ref[...] = jnp.zeros(s_ref.shape, F32)
            kbuf_ref[0:WINDOW, :] = jnp.zeros((WINDOW, kvw), F32)
            vbuf_ref[0:WINDOW, :] = jnp.zeros((WINDOW, kvw), F32)

    class Proj:
        def __init__(self, z_dst, b_dst, h_dst, flag_idx):
            self.z_dst, self.b_dst, self.h_dst, self.flag_idx = z_dst, b_dst, h_dst, flag_idx
            self.z_cols = list(range(0, z_w, zc))

        def norm(self):
            g_mix = gmix_ref[...]
            for r in range(0, t_tile, rb):
                x = h_ref[0, r:r + rb, :]
                self.h_dst[r:r + rb, :] = x
                un_ref[r:r + rb, :] = _rms_rows(x, g_mix).astype(BF16)

        def zproj(self, n=1):
            for _ in range(n):
                if self.z_cols:
                    c0 = self.z_cols.pop(0)
                    w = wina_ref[:, c0:c0 + zc] if c0 < c_qb else winb_ref[:, c0 - c_qb:c0 - c_qb + zc]
                    self.z_dst[:, c0:c0 + zc] = _dot(un_ref[...], w)

        def low_rank(self):
            self.alr = _dot(un_ref[...], walr_ref[...]).astype(BF16)

        def log_decay(self):
            self.la = _log_sigmoid(_dot(self.alr, wa2_ref[...]) + ba_ref[...]) * (LOG2E / GLA_TAU)

        def cumulate(self):
            lows = []
            for r in range(0, t_tile, CHUNK):
                la_c = self.la[r:r + CHUNK, :]
                hi = la_c.astype(BF16)
                lo = (la_c - hi.astype(F32)).astype(BF16)
                b = _dot(tri, hi) + _dot(tri, lo)
                self.b_dst[r:r + CHUNK, :] = b
                lows.append(b[CHUNK - 1:CHUNK, :])
            low = functools.reduce(jnp.minimum, lows)
            flag_ref[self.flag_idx] = (jnp.min(low) >= GLA_SAFE_LOG_DECAY * LOG2E).astype(jnp.int32)

        def rest(self):
            self.zproj(len(self.z_cols))

        def all_rolled(self):
            g_mix = gmix_ref[...]

            def norm_body(i, carry):
                r = pl.multiple_of(i * rb, rb)
                x = h_ref[0, pl.ds(r, rb), :]
                self.h_dst[pl.ds(r, rb), :] = x
                un_ref[pl.ds(r, rb), :] = _rms_rows(x, g_mix).astype(BF16)
                return carry

            lax.fori_loop(0, t_tile // rb, norm_body, 0)
            for w_ref, base in ((wina_ref, 0), (winb_ref, c_qb)):
                def z_body(i, carry, w_ref=w_ref, base=base):
                    c0 = pl.multiple_of(i * zc, zc)
                    self.z_dst[:, pl.ds(base + c0, zc)] = _dot(un_ref[...], w_ref[:, pl.ds(c0, zc)])
                    return carry

                lax.fori_loop(0, w_ref.shape[1] // zc, z_body, 0)
            self.low_rank()
            self.log_decay()
            self.b_dst[...] = self.la

            def cum_body(c, low):
                rows = pl.ds(pl.multiple_of(c * CHUNK, CHUNK), CHUNK)
                la_c = self.b_dst[rows, :]
                hi = la_c.astype(BF16)
                b = _dot(tri, hi) + _dot(tri, (la_c - hi.astype(F32)).astype(BF16))
                self.b_dst[rows, :] = b
                return jnp.minimum(low, b[CHUNK - 1:CHUNK, :])

            low = lax.fori_loop(0, n_chunks, cum_body, jnp.zeros((1, gk), F32))
            flag_ref[self.flag_idx] = (jnp.min(low) >= GLA_SAFE_LOG_DECAY * LOG2E).astype(jnp.int32)

    class Chunk:
        def __init__(self, z_src, b_src, r0, c, exact, masked):
            self.z, self.b_src, self.r0, self.c, self.exact, self.masked = z_src, b_src, r0, c, exact, masked
            self.rows = pl.ds(r0, CHUNK)

        def pre(self):
            z, rows = self.z, self.rows
            b = self.b_src[rows, :]
            b_last = b[CHUNK - 1:CHUNK, :]
            q = z[rows, c_qa:c_qa + gk] * (dk ** -0.5)
            k = z[rows, c_ka:c_ka + gk]
            qg = q * jnp.exp2(b)
            kdec = k * jnp.exp2(b_last - b)
            a_last = jnp.exp2(b_last)
            self.b, self.q, self.qgb = b, q, qg.astype(BF16)
            if not self.exact:
                self.kt = (k * jnp.exp2(-b)).astype(BF16)
            pad = jnp.zeros((LANES - CHUNK - SUBLANES, dk), F32)
            self.kdT, self.acol, self.v = [], [], []
            for hh in range(GLA_HEADS):
                stacked = jnp.concatenate(
                    [kdec[:, hh * dk:(hh + 1) * dk],
                     jnp.broadcast_to(a_last[:, hh * dk:(hh + 1) * dk], (SUBLANES, dk)), pad], axis=0)
                st = stacked.T
                self.kdT.append(st[:, 0:CHUNK].astype(BF16))
                self.acol.append(st[:, CHUNK:CHUNK + 1])
                self.v.append(z[rows, c_va + hh * dv:c_va + (hh + 1) * dv].astype(BF16))
            band = pl.ds(self.r0, BAND)
            self.kb, self.vb, self.qs = [], [], []
            for grp in range(SWA_KV_HEADS):
                self.kb.append(kbuf_ref[band, grp * hd:(grp + 1) * hd].astype(BF16))
                self.vb.append(vbuf_ref[band, grp * hd:(grp + 1) * hd].astype(BF16))
                qgrp = z[rows, c_qb + grp * gw:c_qb + (grp + 1) * gw] * (hd ** -0.5 * LOG2E)
                qs = jnp.concatenate([qgrp[:, m * hd:(m + 1) * hd] for m in range(SWA_GROUP)], axis=0)
                self.qs.append(qs.astype(BF16))

        def mm1(self):
            if self.exact:
                q, b, r0, z = self.q, self.b, self.r0, self.z

                def key_body(j, accs):
                    bj = self.b_src[pl.ds(r0 + j, 1), :]
                    kj = z[pl.ds(r0 + j, 1), c_ka:c_ka + gk]
                    w = q * jnp.exp2(jnp.minimum(b - bj, 0.0)) * kj
                    out = []
                    for hh in range(GLA_HEADS):
                        colsum = jnp.sum(w[:, hh * dk:(hh + 1) * dk], axis=1, keepdims=True)
                        out.append(jnp.where((col == j) & causal, colsum, accs[hh]))
                    return tuple(out)

                self.scores = lax.fori_loop(0, CHUNK, key_body,
                                            tuple(jnp.zeros((CHUNK, CHUNK), F32) for _ in range(GLA_HEADS)))
            else:
                self.scores = [_dot_nt(self.qgb[:, hh * dk:(hh + 1) * dk], self.kt[:, hh * dk:(hh + 1) * dk])
                               for hh in range(GLA_HEADS)]
            self.s_qk = [_dot_nt(self.qs[grp], self.kb[grp]) for grp in range(SWA_KV_HEADS)]
            self.o_inter = []
            for hh in range(GLA_HEADS):
                s_old = s_ref[hh]
                self.o_inter.append(_dot(self.qgb[:, hh * dk:(hh + 1) * dk], s_old.astype(BF16)))
                s_ref[hh] = s_old * self.acol[hh] + _dot(self.kdT[hh], self.v[hh])

        def post1(self):
            if not self.exact:
                self.scores = [jnp.where(causal, s, 0.0) for s in self.scores]
            self.scores = [s.astype(BF16) for s in self.scores]
            if self.masked:
                cg = t_mix * n_chunks + self.c
                kcol = lax.broadcasted_iota(jnp.int32, (CHUNK, BAND), 1)
                valid = kcol >= WINDOW - CHUNK * jnp.minimum(cg, WINDOW // CHUNK)
            self.p, self.den = [], []
            for grp in range(SWA_KV_HEADS):
                ps, dens = [], []
                for m in range(SWA_GROUP):
                    sm = self.s_qk[grp][m * CHUNK:(m + 1) * CHUNK, :]
                    if self.masked:
                        sm = jnp.where(valid, sm, -jnp.inf)
                    sink = sink_ref[grp * SWA_GROUP + m] * LOG2E
                    mx = jnp.maximum(jnp.max(sm, axis=1, keepdims=True), sink)
                    pm = jnp.exp2(sm - mx)
                    dens.append(jnp.sum(pm, axis=1, keepdims=True) + jnp.exp2(sink - mx))
                    ps.append(pm.astype(BF16))
                self.p.append(jnp.concatenate(ps, axis=0))
                self.den.append(dens)

        def mm2(self):
            self.o = [_dot(self.scores[hh], self.v[hh]) + self.o_inter[hh] for hh in range(GLA_HEADS)]
            self.og = [_dot(self.p[grp], self.vb[grp]) for grp in range(SWA_KV_HEADS)]

        def post2(self):
            z, rows = self.z, self.rows
            g_gla = ggla_ref[...]
            for j in range(GLA_HEADS):
                o = _rms_rows(self.o[j], g_gla[:, j * dv:(j + 1) * dv])
                ra = z[rows, c_ra + j * dv:c_ra + (j + 1) * dv]
                ga = z[rows, c_ga + j * dv:c_ga + (j + 1) * dv]
                gb = z[rows, c_gb + j * gw:c_gb + (j + 1) * gw]
                ob = jnp.concatenate([self.og[j][m * CHUNK:(m + 1) * CHUNK, :] / self.den[j][m]
                                      for m in range(SWA_GROUP)], axis=1)
                mg_ref[rows, j * dv:(j + 1) * dv] = (
                    _sigmoid(ga) * (o * (ra * _sigmoid(ra))) + _sigmoid(gb) * ob).astype(BF16)

    def stage_kv(z_src):
        kbuf_ref[WINDOW:WINDOW + t_tile, :] = z_src[:, c_kb:c_kb + kvw]
        vbuf_ref[WINDOW:WINDOW + t_tile, :] = z_src[:, c_vb:c_vb + kvw]

    def out_proj(h_src, rolled=False):
        def body(n, carry):
            cols = pl.ds(n * MXU_N, MXU_N) if isinstance(n, int) else pl.ds(pl.multiple_of(n * MXU_N, MXU_N), MXU_N)
            o_ref[0, :, cols] = h_src[:, cols] + _dot(mg_ref[...], wout_ref[:, cols])
            return carry

        if rolled:
            lax.fori_loop(0, d // MXU_N, body, 0)
        else:
            for n in range(d // MXU_N):
                body(n, 0)

    def step(slot):
        z_dst, b_dst, h_dst = z_refs[slot], b_refs[slot], h_refs[slot]
        z_src, b_src, h_src = z_refs[1 - slot], b_refs[1 - slot], h_refs[1 - slot]
        safe = flag_ref[1 - slot] == 1

        @pl.when(safe)
        def _fused():
            proj = Proj(z_dst, b_dst, h_dst, slot)
            stage_kv(z_src)
            for c in range(n_chunks):
                ch = Chunk(z_src, b_src, c * CHUNK, c, exact=False,
                           masked=(not has_cache) and c < WINDOW // CHUNK)
                ch.pre()
                if c == 0:
                    ch.mm1()
                    proj.norm()
                    proj.zproj()
                else:
                    proj.zproj()
                    ch.mm1()
                if c == 0:
                    proj.low_rank()
                proj.zproj(2)
                if c == min(1, n_chunks - 1):
                    proj.log_decay()
                ch.post1()
                ch.mm2()
                ch.post2()
            proj.rest()
            proj.cumulate()
            out_proj(h_src)

        @pl.when(jnp.logical_not(safe))
        def _exact():
            Proj(z_dst, b_dst, h_dst, slot).all_rolled()
            stage_kv(z_src)

            def chunk_body(c, carry):
                ch = Chunk(z_src, b_src, pl.multiple_of(c * CHUNK, CHUNK), c, exact=True, masked=not has_cache)
                ch.pre()
                ch.mm1()
                ch.post1()
                ch.mm2()
                ch.post2()
                return carry

            lax.fori_loop(0, n_chunks, chunk_body, 0)
            out_proj(h_src, rolled=True)

    parity = lax.rem(g, 2)
    pl.when(parity == 0)(functools.partial(step, 0))
    pl.when(parity == 1)(functools.partial(step, 1))

    k_tail = kbuf_ref[t_tile:t_tile + WINDOW, :]
    v_tail = vbuf_ref[t_tile:t_tile + WINDOW, :]
    kbuf_ref[0:WINDOW, :] = k_tail
    vbuf_ref[0:WINDOW, :] = v_tail

    @pl.when(t_mix == n_t - 1)
    def _emit():
        knew_ref[0] = k_tail
        vnew_ref[0] = v_tail
        sfin_ref[0] = s_ref[...]


def _mixer_call(h, cache, gmix, win, winb, walr, wa2, ba, ggla, sinks, wout, layer, *, dk, dv, hd):
    bsz, seq, d = h.shape
    t_tile = min(seq, MIXER_ROW_TILE)
    assert seq % t_tile == 0 and t_tile % CHUNK == 0
    n_t = seq // t_tile
    n_tiles = bsz * n_t
    has_cache = cache is not None
    kvw = SWA_KV_HEADS * hd
    gk = GLA_HEADS * dk
    n_a = 4 * d + 2 * gk
    z_w = n_a + winb.shape[2]
    assert n_a % (2 * MXU_N) == 0 and winb.shape[2] % (2 * MXU_N) == 0
    assert GLA_HEADS * dv == d and SWA_HEADS * hd == d and SWA_GROUP * hd == dv

    def proj_tile(g):
        q = jnp.minimum(g, n_tiles - 1)
        return (q // n_t, q % n_t, 0)

    def mix_tile(g):
        q = jnp.maximum(g - 1, 0)
        return (q // n_t, q % n_t, 0)

    per_seq = lambda *shape: pl.BlockSpec((1,) + shape, lambda g: (jnp.maximum(g - 1, 0) // n_t,) + (0,) * len(shape))
    args, specs = [h], [pl.BlockSpec((1, t_tile, d), proj_tile)]
    if has_cache:
        ck, cv, s0 = cache
        args += [ck, cv, s0]
        specs += [per_seq(WINDOW, kvw), per_seq(WINDOW, kvw), per_seq(GLA_HEADS, dk, dv)]
    args += [gmix.reshape(1, d), win, winb, walr, wa2, ba.reshape(1, gk), ggla.reshape(1, d), sinks, wout]
    specs += [_const_spec((1, d)), _layer_spec(win, layer, cols=n_a), _layer_spec(winb, layer),
              _layer_spec(walr, layer), _layer_spec(wa2, layer),
              _const_spec((1, gk)), _const_spec((1, d)), pl.BlockSpec(memory_space=pltpu.SMEM),
              _layer_spec(wout, layer)]
    out_shape = (jax.ShapeDtypeStruct((bsz, seq, d), F32),
                 jax.ShapeDtypeStruct((bsz, WINDOW, kvw), F32),
                 jax.ShapeDtypeStruct((bsz, WINDOW, kvw), F32),
                 jax.ShapeDtypeStruct((bsz, GLA_HEADS, dk, dv), F32))
    out_specs = (pl.BlockSpec((1, t_tile, d), mix_tile), per_seq(WINDOW, kvw), per_seq(WINDOW, kvw),
                 per_seq(GLA_HEADS, dk, dv))
    scratch = [pltpu.VMEM((t_tile, d), BF16)]
    scratch += [pltpu.VMEM((t_tile, z_w), F32)] * 2
    scratch += [pltpu.VMEM((t_tile, gk), F32)] * 2
    scratch += [pltpu.VMEM((t_tile, d), F32)] * 2
    scratch += [
        pltpu.VMEM((t_tile, d), BF16),
        pltpu.VMEM((WINDOW + t_tile, kvw), F32),
        pltpu.VMEM((WINDOW + t_tile, kvw), F32),
        pltpu.VMEM((GLA_HEADS, dk, dv), F32),
        pltpu.SMEM((2,), jnp.int32),
    ]
    return pl.pallas_call(
        functools.partial(_mixer_kernel, t_tile=t_tile, n_t=n_t, d=d, dk=dk, dv=dv, hd=hd, has_cache=has_cache),
        out_shape=out_shape,
        grid=(n_tiles + 1,),
        in_specs=specs,
        out_specs=out_specs,
        scratch_shapes=scratch,
        compiler_params=pltpu.CompilerParams(
            dimension_semantics=("arbitrary",), vmem_limit_bytes=VMEM_LIMIT_BYTES),
        name="mixer_cache" if has_cache else "mixer",
    )(*args)


def _prep_weights(d, dk, w_ffn1_up, w_ffn1_down, w_in, w_gla_a2, w_out, w_ffn2_up, w_ffn2_down, w_ple_gate, w_ple):
    n_a = 4 * d + 2 * GLA_HEADS * dk
    return dict(
        up1=w_ffn1_up.astype(BF16), dn1=w_ffn1_down.astype(BF16),
        up2=w_ffn2_up.astype(BF16), dn2=w_ffn2_down.astype(BF16),
        win=w_in.astype(BF16), winb=w_in[:, :, n_a + GLA_RANK:].astype(BF16),
        walr=jnp.pad(w_in[:, :, n_a:n_a + GLA_RANK], ((0, 0), (0, 0), (0, LANES - GLA_RANK))).astype(BF16),
        wa2=jnp.pad(w_gla_a2, ((0, 0), (0, LANES - GLA_RANK), (0, 0))).astype(BF16),
        wout=w_out.astype(BF16), wpg=w_ple_gate.astype(BF16), wpl=w_ple.astype(BF16))


def kernel(x_prompt, x_sample, p_prompt, p_sample, cache_swa_k, cache_swa_v, state_gla, g_ffn1, w_ffn1_up, w_ffn1_down, g_mix, w_in, w_gla_a2, b_gla_a, g_gla, swa_sinks, w_out, g_ffn2, w_ffn2_up, w_ffn2_down, g_ple, w_ple_gate, w_ple, g_final):
    depth = w_in.shape[0]
    bp, sp, d = x_prompt.shape
    bs, ss, _ = x_sample.shape
    dk = state_gla.shape[-2]
    dv = state_gla.shape[-1]
    hd = cache_swa_k.shape[-1]
    kvw = SWA_KV_HEADS * hd
    wc = cache_swa_k.shape[2]
    assert wc == WINDOW and sp >= WINDOW

    xp = x_prompt.reshape(bp * sp, d)
    xs = x_sample.reshape(bs * ss, d)
    pe_p = p_prompt.reshape(depth, bp * sp, -1)
    pe_s = p_sample.reshape(depth, bs * ss, -1)
    w = _prep_weights(d, dk, w_ffn1_up, w_ffn1_down, w_in, w_gla_a2, w_out, w_ffn2_up, w_ffn2_down, w_ple_gate, w_ple)
    outs = [[] for _ in range(6)]
    for i in range(depth):
        last = g_final if i == depth - 1 else None
        mix = functools.partial(_mixer_call, gmix=g_mix[i], win=w["win"], winb=w["winb"], walr=w["walr"], wa2=w["wa2"],
                                ba=b_gla_a[i], ggla=g_gla[i], sinks=swa_sinks[i], wout=w["wout"], layer=i,
                                dk=dk, dv=dv, hd=hd)
        xp = _ffn_call(xp, g_ffn1[i], w["up1"], w["dn1"], i)
        xp3, pk, pv, ps = mix(xp.reshape(bp, sp, d), None)
        xp = _ffn_call(xp3.reshape(bp * sp, d), g_ffn2[i], w["up2"], w["dn2"], i,
                       ple=(pe_p, g_ple[i], w["wpg"], w["wpl"]), g_final=last)
        xs = _ffn_call(xs, g_ffn1[i], w["up1"], w["dn1"], i)
        cache = (cache_swa_k[i].reshape(bs, wc, kvw), cache_swa_v[i].reshape(bs, wc, kvw), state_gla[i])
        xs3, sk, sv, s_s = mix(xs.reshape(bs, ss, d), cache)
        xs = _ffn_call(xs3.reshape(bs * ss, d), g_ffn2[i], w["up2"], w["dn2"], i,
                       ple=(pe_s, g_ple[i], w["wpg"], w["wpl"]), g_final=last)
        for lst, val in zip(outs, (pk.reshape(bp, WINDOW, SWA_KV_HEADS, hd), pv.reshape(bp, WINDOW, SWA_KV_HEADS, hd),
                                   ps, sk.reshape(bs, wc, SWA_KV_HEADS, hd), sv.reshape(bs, wc, SWA_KV_HEADS, hd), s_s)):
            lst.append(val)
    return (xp.reshape(bp, sp, d), xs.reshape(bs, ss, d)) + tuple(jnp.stack(o) for o in outs)
```
